```python
import math
import jax, jax.numpy as jnp
from jax import lax
import numpy as np

D_MODEL = 2048
BATCH = 2
SEQ = 16384
DEPTH = 1
DEC_BATCH = 8
DEC_SEQ = 2048
PAST_LEN = 128

N_HEADS = 16
QK_NOPE_DIM = 128
QK_ROPE_DIM = 64
V_HEAD_DIM = 128
Q_LORA_RANK = 512
KV_LORA_RANK = 512
ROPE_THETA = 10000.0
Q_BLOCK = 128
ATTN_SCALE = (QK_NOPE_DIM + QK_ROPE_DIM) ** -0.5
SSM_WIDTH = D_MODEL // 2
SSM_GROUP = 16
SSM_GROUPS = SSM_WIDTH // SSM_GROUP
SSM_STATE = 64
SSM_DIRS = 2
DT_MIN = 0.001
DT_MAX = 0.1
FF_DIM = 4 * D_MODEL
PLE_DIM = 256
NORM_EPS = 1e-6
C_Q = Q_LORA_RANK
C_KV = KV_LORA_RANK
C_KR = QK_ROPE_DIM
C_SSM = SSM_WIDTH
C_GATE = D_MODEL
IN_COLS = C_Q + C_KV + C_KR + C_SSM + 2 * C_GATE
IN_SPLITS = [C_Q, C_Q + C_KV, C_Q + C_KV + C_KR, C_Q + C_KV + C_KR + C_SSM, C_Q + C_KV + C_KR + C_SSM + C_GATE]

kernel_name = "hybrid_mla_s5_bidir_encoder"


def rmsnorm(x, g):
    xf = x.astype(jnp.float32)
    r = lax.rsqrt(jnp.mean(xf * xf, axis=-1, keepdims=True) + NORM_EPS)
    return (xf * r).astype(x.dtype) * g


def rope_tables(length, dtype):
    pos = jnp.arange(length, dtype=jnp.float32)
    inv = ROPE_THETA ** (-jnp.arange(0, QK_ROPE_DIM, 2, dtype=jnp.float32) / QK_ROPE_DIM)
    ang = pos[:, None] * inv[None, :]
    return jnp.cos(ang).astype(dtype), jnp.sin(ang).astype(dtype)


def apply_rope(x, cos, sin):
    half = QK_ROPE_DIM // 2
    x1, x2 = x[..., :half], x[..., half:]
    return jnp.concatenate([x1 * cos - x2 * sin, x2 * cos + x1 * sin], axis=-1)


def mla_branch(c_q, c_kv, k_r, norm_q, w_uq, norm_kv, w_uk, w_uv, w_o):
    b, length, _ = c_q.shape
    q = (rmsnorm(c_q, norm_q) @ w_uq).reshape(b, length, N_HEADS, QK_NOPE_DIM + QK_ROPE_DIM)
    q_nope, q_rope = q[..., :QK_NOPE_DIM], q[..., QK_NOPE_DIM:]
    c_kv = rmsnorm(c_kv, norm_kv)
    cos, sin = rope_tables(length, q.dtype)
    q_rope = apply_rope(q_rope, cos[:, None, :], sin[:, None, :])
    k_r = apply_rope(k_r, cos, sin)
    nblk = length // Q_BLOCK
    qn = q_nope.reshape(b, nblk, Q_BLOCK, N_HEADS, QK_NOPE_DIM).transpose(1, 0, 2, 3, 4)
    qr = q_rope.reshape(b, nblk, Q_BLOCK, N_HEADS, QK_ROPE_DIM).transpose(1, 0, 2, 3, 4)

    def one_block(blk):
        qn_b, qr_b = blk
        q_lat = jnp.einsum('bqhd,chd->bqhc', qn_b, w_uk)
        s = jnp.einsum('bqhc,bkc->bhqk', q_lat, c_kv) + jnp.einsum('bqhr,bkr->bhqk', qr_b, k_r)
        p = jax.nn.softmax(s.astype(jnp.float32) * ATTN_SCALE, axis=-1).astype(c_kv.dtype)
        ctx = jnp.einsum('bhqk,bkc->bqhc', p, c_kv)
        return jnp.einsum('bqhc,chv->bqhv', ctx, w_uv)

    o = lax.map(one_block, (qn, qr))
    o = o.transpose(1, 0, 2, 3, 4).reshape(b, length, N_HEADS * V_HEAD_DIM)
    return o @ w_o


def _ssm_combine(e1, e2):
    a1r, a1i, b1r, b1i = e1
    a2r, a2i, b2r, b2i = e2
    return (a2r * a1r - a2i * a1i,
            a2r * a1i + a2i * a1r,
            a2r * b1r - a2i * b1i + b2r,
            a2r * b1i + a2i * b1r + b2i)


def s5_branch(u, a_re, a_im, log_step, b_re, b_im, c_re, c_im, d_skip, w_glu):
    b, length, _ = u.shape
    ug = u.reshape(b, length, SSM_GROUPS, SSM_GROUP)
    y = d_skip * u
    for dr in range(SSM_DIRS):
        step = jnp.exp(log_step[dr])[:, None]
        lr, li = a_re[dr], a_im[dr]
        mag = jnp.exp(step * lr)
        ab_re, ab_im = mag * jnp.cos(step * li), mag * jnp.sin(step * li)
        den = lr * lr + li * li
        nr, ni = ab_re - 1.0, ab_im
        f_re = (nr * lr + ni * li) / den
        f_im = (ni * lr - nr * li) / den
        bb_re = f_re[..., None] * b_re[dr] - f_im[..., None] * b_im[dr]
        bb_im = f_re[..., None] * b_im[dr] + f_im[..., None] * b_re[dr]
        ud = ug if dr == 0 else jnp.flip(ug, axis=1)
        bu_re = jnp.einsum('blgp,gnp->blgn', ud, bb_re)
        bu_im = jnp.einsum('blgp,gnp->blgn', ud, bb_im)
        a_full_re = jnp.broadcast_to(ab_re, bu_re.shape)
        a_full_im = jnp.broadcast_to(ab_im, bu_im.shape)
        _, _, x_re, x_im = lax.associative_scan(_ssm_combine, (a_full_re, a_full_im, bu_re, bu_im), axis=1)
        yd = jnp.einsum('blgn,gpn->blgp', x_re, c_re[dr]) - jnp.einsum('blgn,gpn->blgp', x_im, c_im[dr])
        if dr == 1:
            yd = jnp.flip(yd, axis=1)
        y = y + yd.reshape(b, length, SSM_WIDTH)
    z = jax.nn.gelu(y) @ w_glu
    return z[..., :D_MODEL] * jax.nn.sigmoid(z[..., D_MODEL:])


def encoder(x, p, norm_mix, w_in, norm_q, w_uq, norm_kv, w_uk, w_uv, w_o_attn,
            ssm_a_re, ssm_a_im, ssm_log_step, ssm_b_re, ssm_b_im, ssm_c_re, ssm_c_im, ssm_d, w_glu,
            w_out, norm_mlp, w_mlp_in, w_mlp_out, norm_ple, w_ple_gate, w_ple, norm_final):
    h = x
    for i in range(DEPTH):
        n = rmsnorm(h, norm_mix[i])
        proj = n @ w_in[i]
        c_q, c_kv, k_r, u, g_a, g_s = jnp.split(proj, IN_SPLITS, axis=-1)
        o_a = mla_branch(c_q, c_kv, k_r, norm_q[i], w_uq[i], norm_kv[i], w_uk[i], w_uv[i], w_o_attn[i])
        o_s = s5_branch(u, ssm_a_re[i], ssm_a_im[i], ssm_log_step[i], ssm_b_re[i], ssm_b_im[i],
                        ssm_c_re[i], ssm_c_im[i], ssm_d[i], w_glu[i])
        mix = jax.nn.sigmoid(g_a) * o_a + jax.nn.sigmoid(g_s) * o_s
        h = h + mix @ w_out[i]
        n = rmsnorm(h, norm_mlp[i])
        h = h + jnp.square(jax.nn.relu(n @ w_mlp_in[i])) @ w_mlp_out[i]
        gate = jax.nn.sigmoid(rmsnorm(h, norm_ple[i]) @ w_ple_gate[i])
        h = h + gate * (p[i] @ w_ple[i])
    return rmsnorm(h, norm_final)


def setup_inputs(seed: int = 0) -> dict:
    key = jax.random.key(seed)
    ks = jax.random.split(key, 32)

    def nrm(k, shape, scale):
        return jax.random.normal(k, shape, jnp.float32) * scale

    def gain(k, shape):
        return 1.0 + 0.02 * jax.random.normal(k, shape, jnp.float32)

    n_idx = jnp.arange(SSM_STATE, dtype=jnp.float32)
    a_shape = (DEPTH, SSM_DIRS, SSM_GROUPS, SSM_STATE)
    log_step = jax.random.uniform(ks[12], (DEPTH, SSM_DIRS, SSM_GROUPS), jnp.float32,
                                  math.log(DT_MIN), math.log(DT_MAX))
    return {
        "x_prompt": nrm(ks[0], (BATCH, SEQ, D_MODEL), 1.0),
        "x_sample": nrm(ks[1], (DEC_BATCH, DEC_SEQ, D_MODEL), 1.0),
        "p_prompt": nrm(ks[2], (DEPTH, BATCH, SEQ, PLE_DIM), 1.0),
        "p_sample": nrm(ks[3], (DEPTH, DEC_BATCH, DEC_SEQ, PLE_DIM), 1.0),
        "norm_mix": gain(ks[4], (DEPTH, D_MODEL)),
        "w_in": nrm(ks[5], (DEPTH, D_MODEL, IN_COLS), D_MODEL ** -0.5),
        "norm_q": gain(ks[6], (DEPTH, Q_LORA_RANK)),
        "w_uq": nrm(ks[7], (DEPTH, Q_LORA_RANK, N_HEADS * (QK_NOPE_DIM + QK_ROPE_DIM)), Q_LORA_RANK ** -0.5),
        "norm_kv": gain(ks[8], (DEPTH, KV_LORA_RANK)),
        "w_uk": nrm(ks[9], (DEPTH, KV_LORA_RANK, N_HEADS, QK_NOPE_DIM), KV_LORA_RANK ** -0.5),
        "w_uv": nrm(ks[10], (DEPTH, KV_LORA_RANK, N_HEADS, V_HEAD_DIM), KV_LORA_RANK ** -0.5),
        "w_o_attn": nrm(ks[11], (DEPTH, N_HEADS * V_HEAD_DIM, D_MODEL), (N_HEADS * V_HEAD_DIM) ** -0.5),
        "ssm_a_re": -0.5 + 0.01 * jax.random.normal(ks[13], a_shape, jnp.float32),
        "ssm_a_im": math.pi * n_idx + 0.01 * jax.random.normal(ks[14], a_shape, jnp.float32),
        "ssm_log_step": log_step,
        "ssm_b_re": nrm(ks[15], (DEPTH, SSM_DIRS, SSM_GROUPS, SSM_STATE, SSM_GROUP), (2 * SSM_GROUP) ** -0.5),
        "ssm_b_im": nrm(ks[16], (DEPTH, SSM_DIRS, SSM_GROUPS, SSM_STATE, SSM_GROUP), (2 * SSM_GROUP) ** -0.5),
        "ssm_c_re": nrm(ks[17], (DEPTH, SSM_DIRS, SSM_GROUPS, SSM_GROUP, SSM_STATE), SSM_STATE ** -0.5),
        "ssm_c_im": nrm(ks[18], (DEPTH, SSM_DIRS, SSM_GROUPS, SSM_GROUP, SSM_STATE), SSM_STATE ** -0.5),
        "ssm_d": nrm(ks[19], (DEPTH, SSM_WIDTH), 1.0),
        "w_glu": nrm(ks[20], (DEPTH, SSM_WIDTH, 2 * D_MODEL), SSM_WIDTH ** -0.5),
        "w_out": nrm(ks[21], (DEPTH, D_MODEL, D_MODEL), D_MODEL ** -0.5),
        "norm_mlp": gain(ks[22], (DEPTH, D_MODEL)),
        "w_mlp_in": nrm(ks[23], (DEPTH, D_MODEL, FF_DIM), D_MODEL ** -0.5),
        "w_mlp_out": nrm(ks[24], (DEPTH, FF_DIM, D_MODEL), FF_DIM ** -0.5),
        "norm_ple": gain(ks[25], (DEPTH, D_MODEL)),
        "w_ple_gate": nrm(ks[26], (DEPTH, D_MODEL, D_MODEL), D_MODEL ** -0.5),
        "w_ple": nrm(ks[27], (DEPTH, PLE_DIM, D_MODEL), PLE_DIM ** -0.5),
        "norm_final": gain(ks[28], (D_MODEL,)),
    }


def reference(x_prompt, x_sample, p_prompt, p_sample, norm_mix, w_in, norm_q, w_uq, norm_kv, w_uk, w_uv,
              w_o_attn, ssm_a_re, ssm_a_im, ssm_log_step, ssm_b_re, ssm_b_im, ssm_c_re, ssm_c_im, ssm_d,
              w_glu, w_out, norm_mlp, w_mlp_in, w_mlp_out, norm_ple, w_ple_gate, w_ple, norm_final):
    y_prompt = encoder(x_prompt, p_prompt, norm_mix, w_in, norm_q, w_uq, norm_kv, w_uk, w_uv, w_o_attn,
                       ssm_a_re, ssm_a_im, ssm_log_step, ssm_b_re, ssm_b_im, ssm_c_re, ssm_c_im, ssm_d, w_glu,
                       w_out, norm_mlp, w_mlp_in, w_mlp_out, norm_ple, w_ple_gate, w_ple, norm_final)
    y_sample = encoder(x_sample, p_sample, norm_mix, w_in, norm_q, w_uq, norm_kv, w_uk, w_uv, w_o_attn,
                       ssm_a_re, ssm_a_im, ssm_log_step, ssm_b_re, ssm_b_im, ssm_c_re, ssm_c_im, ssm_d, w_glu,
                       w_out, norm_mlp, w_mlp_in, w_mlp_out, norm_ple, w_ple_gate, w_ple, norm_final)
    return (y_prompt, y_sample)
```

```python
import functools
import math

import jax
import jax.numpy as jnp
from jax import lax
from jax.experimental import pallas as pl
from jax.experimental.pallas import tpu as pltpu

NORM_EPS = 1e-6
ROPE_THETA = 10000.0
LANES = 128
SUBLANES = 8
MXU_DIM = 256
VMEM_LIMIT = 56 * 1024 * 1024
SSM_CHUNK = 32
BF16 = jnp.bfloat16
F32 = jnp.float32


def _params(n_axes, n_parallel=None):
    n_parallel = n_axes if n_parallel is None else n_parallel
    sem = ("parallel",) * n_parallel + ("arbitrary",) * (n_axes - n_parallel)
    return pltpu.CompilerParams(dimension_semantics=sem, vmem_limit_bytes=VMEM_LIMIT)


def _resident(shape):
    nd = len(shape)
    return pl.BlockSpec(shape, lambda *_: (0,) * nd, pipeline_mode=pl.Buffered(1))


def _rms(x, g):
    r = lax.rsqrt(jnp.mean(x * x, axis=-1, keepdims=True) + NORM_EPS)
    return (x * r) * g


def _dot(a, b):
    return jnp.dot(a, b, preferred_element_type=F32)


def _dot_nt(a, b):
    return lax.dot_general(a, b, (((1,), (1,)), ((), ())), preferred_element_type=F32)


def _norm_cast_kernel(x_ref, g_ref, o_ref):
    o_ref[...] = _rms(x_ref[...], g_ref[...]).astype(BF16)


def norm_cast(x, g, tm):
    t, d = x.shape
    return pl.pallas_call(
        _norm_cast_kernel,
        grid=(t // tm,),
        in_specs=[pl.BlockSpec((tm, d), lambda i: (i, 0)), _resident((1, d))],
        out_specs=pl.BlockSpec((tm, d), lambda i: (i, 0)),
        out_shape=jax.ShapeDtypeStruct((t, d), BF16),
        compiler_params=_params(1),
        name="norm_cast",
    )(x, g.reshape(1, d))


def _mm_kernel(a_ref, w_ref, o_ref):
    o_ref[...] = _dot(a_ref[...], w_ref[...]).astype(o_ref.dtype)


def mm(a, w, tm, tn, out_dtype):
    m, k = a.shape
    n = w.shape[1]
    return pl.pallas_call(
        _mm_kernel,
        grid=(m // tm, n // tn),
        in_specs=[pl.BlockSpec((tm, k), lambda i, j: (i, 0)), pl.BlockSpec((k, tn), lambda i, j: (0, j))],
        out_specs=pl.BlockSpec((tm, tn), lambda i, j: (i, j)),
        out_shape=jax.ShapeDtypeStruct((m, n), out_dtype),
        compiler_params=_params(2),
        name="mm",
    )(a, w)


def _qkv_kernel(xn_ref, wq_ref, wkv_ref, wkr_ref, gq_ref, gkv_ref, cos_ref, sin_ref,
                qn_ref, ckv_ref, kr_ref):
    xn = xn_ref[...]
    qn_ref[...] = _rms(_dot(xn, wq_ref[...]), gq_ref[...]).astype(BF16)
    ckv_ref[...] = _rms(_dot(xn, wkv_ref[...]), gkv_ref[...]).astype(BF16)
    kr = _dot(xn, wkr_ref[...])
    kr_ref[...] = (kr * cos_ref[...] + pltpu.roll(kr, LANES // 2, 1) * sin_ref[...]).astype(BF16)


def qkv_proj(xn, wq, wkv, wkr, gq, gkv, cos_l, sin_l, seq_len, tm):
    t, d = xn.shape
    cq, ckv = wq.shape[1], wkv.shape[1]
    npos = seq_len // tm
    row = lambda i: (i, 0)
    pos = lambda i: (i % npos, 0)
    return pl.pallas_call(
        _qkv_kernel,
        grid=(t // tm,),
        in_specs=[pl.BlockSpec((tm, d), row), _resident((d, cq)), _resident((d, ckv)), _resident((d, LANES)),
                  _resident((1, cq)), _resident((1, ckv)),
                  pl.BlockSpec((tm, LANES), pos), pl.BlockSpec((tm, LANES), pos)],
        out_specs=[pl.BlockSpec((tm, cq), row), pl.BlockSpec((tm, ckv), row), pl.BlockSpec((tm, LANES), row)],
        out_shape=[jax.ShapeDtypeStruct((t, cq), BF16), jax.ShapeDtypeStruct((t, ckv), BF16),
                   jax.ShapeDtypeStruct((t, LANES), BF16)],
        compiler_params=_params(1),
        name="qkv_proj",
    )(xn, wq, wkv, wkr, gq.reshape(1, cq), gkv.reshape(1, ckv), cos_l, sin_l)


def _attn_prep_kernel(qn_ref, ckv_ref, kr_ref, wuq_ref, wuk_ref, wuv_ref, cos_ref, sin_ref,
                      qt_ref, k_ref, vt_ref, *, n_heads, dn):
    qn = qn_ref[...]
    ckv = ckv_ref[...]
    cos_t = cos_ref[...]
    sin_t = sin_ref[...]
    half = LANES // 2
    for h in range(n_heads):
        q = _dot_nt(wuq_ref[h], qn)
        qr = q[dn:]
        qr_sw = jnp.concatenate([qr[half:], qr[:half]], axis=0)
        qt_ref[0, h, :dn, :] = q[:dn].astype(BF16)
        qt_ref[0, h, dn:, :] = (qr * cos_t + qr_sw * sin_t).astype(BF16)
    k_all = _dot(ckv, wuk_ref[...])
    kr = kr_ref[...]
    for h in range(n_heads):
        k_ref[0, h, :, :dn] = k_all[:, h * dn:(h + 1) * dn].astype(BF16)
        k_ref[0, h, :, dn:] = kr
    vt_ref[0] = _dot_nt(wuv_ref[...], ckv).astype(BF16)


def attn_prep(qn, ckv, kr, wuq_t, wuk, wuv_t, cos_t, sin_t, batch, seq_len, tm):
    n_heads, dk, cq = wuq_t.shape
    dn = dk - LANES
    hdv = wuv_t.shape[0]
    ckv_dim = ckv.shape[1]
    npos = seq_len // tm
    row = lambda b, i: (b * npos + i, 0)
    col = lambda b, i: (0, i)
    return pl.pallas_call(
        functools.partial(_attn_prep_kernel, n_heads=n_heads, dn=dn),
        grid=(batch, npos),
        in_specs=[pl.BlockSpec((tm, cq), row), pl.BlockSpec((tm, ckv_dim), row), pl.BlockSpec((tm, LANES), row),
                  _resident((n_heads, dk, cq)), _resident((ckv_dim, n_heads * dn)), _resident((hdv, ckv_dim)),
                  pl.BlockSpec((LANES, tm), col), pl.BlockSpec((LANES, tm), col)],
        out_specs=[pl.BlockSpec((1, n_heads, dk, tm), lambda b, i: (b, 0, 0, i)),
                   pl.BlockSpec((1, n_heads, tm, dk), lambda b, i: (b, 0, i, 0)),
                   pl.BlockSpec((1, hdv, tm), lambda b, i: (b, 0, i))],
        out_shape=[jax.ShapeDtypeStruct((batch, n_heads, dk, seq_len), BF16),
                   jax.ShapeDtypeStruct((batch, n_heads, seq_len, dk), BF16),
                   jax.ShapeDtypeStruct((batch, hdv, seq_len), BF16)],
        compiler_params=_params(2),
        name="attn_prep",
    )(qn, ckv, kr, wuq_t, wuk, wuv_t, cos_t, sin_t)


def _flash_kernel(qt_ref, k_ref, vt_ref, o_ref, m_ref, l_ref, acc_ref, *, tk, nk):
    qt = qt_ref[0, 0]
    m_ref[...] = jnp.full(m_ref.shape, -jnp.inf, F32)
    l_ref[...] = jnp.zeros(l_ref.shape, F32)
    acc_ref[...] = jnp.zeros(acc_ref.shape, F32)

    def body(j, carry):
        off = pl.multiple_of(j * tk, tk)
        s = _dot(k_ref[0, 0, pl.ds(off, tk), :], qt)
        m_prev = m_ref[...]
        m_new = jnp.maximum(m_prev, jnp.max(s, axis=0, keepdims=True))
        alpha = jnp.exp2(m_prev - m_new)
        p = jnp.exp2(s - m_new)
        l_ref[...] = alpha * l_ref[...] + jnp.sum(p, axis=0, keepdims=True)
        pv = _dot(vt_ref[0, :, pl.ds(off, tk)], p.astype(BF16))
        acc_ref[...] = alpha * acc_ref[...] + pv
        m_ref[...] = m_new
        return carry

    lax.fori_loop(0, nk, body, 0)
    o_ref[0] = (acc_ref[...] / l_ref[...]).T.astype(BF16)


def flash_attn(qt, k, vt, dv, tq, tk):
    batch, n_heads, dk, seq_len = qt.shape
    return pl.pallas_call(
        functools.partial(_flash_kernel, tk=tk, nk=seq_len // tk),
        grid=(batch, n_heads, seq_len // tq),
        in_specs=[pl.BlockSpec((1, 1, dk, tq), lambda b, h, i: (b, h, 0, i)),
                  pl.BlockSpec((1, 1, seq_len, dk), lambda b, h, i: (b, h, 0, 0)),
                  pl.BlockSpec((1, dv, seq_len), lambda b, h, i: (b, h, 0))],
        out_specs=pl.BlockSpec((1, tq, dv), lambda b, h, i: (b, i, h)),
        out_shape=jax.ShapeDtypeStruct((batch, seq_len, n_heads * dv), BF16),
        scratch_shapes=[pltpu.VMEM((1, tq), F32), pltpu.VMEM((1, tq), F32), pltpu.VMEM((dv, tq), F32)],
        compiler_params=_params(3),
        name="flash_attn",
    )(qt, k, vt)


def _rows_to_block(rows):
    n = rows[0].shape[1]
    rid = lax.broadcasted_iota(jnp.int32, (SUBLANES, n), 0)
    blk = jnp.broadcast_to(rows[-1], (SUBLANES, n))
    for k in range(SUBLANES - 2, -1, -1):
        blk = jnp.where(rid == k, rows[k], blk)
    return blk


def _ssm_kernel(u_ref, toep_ref, w1_ref, w2_ref, dec_ref, y_ref, s_ref, xf_ref, xb_ref, *, n_seq, n_chunk, n_state):
    u = u_ref[0]
    s_ref[...] = _dot(u, w1_ref[0])
    dec = dec_ref[0]
    a_re, a_im = dec[0:1], dec[1:2]
    fwd_lane = lax.broadcasted_iota(jnp.int32, (1, LANES), 1) < n_state
    n_blk = n_chunk // SUBLANES

    def step(i, carry):
        new = []
        for q in range(n_seq):
            x_re, x_im = carry[2 * q], carry[2 * q + 1]
            rf = pl.multiple_of(q * n_chunk + i * SUBLANES, SUBLANES)
            rb = pl.multiple_of(q * n_chunk + (n_blk - 1 - i) * SUBLANES, SUBLANES)
            sf = s_ref[pl.ds(rf, SUBLANES), :]
            sb = s_ref[pl.ds(rb, SUBLANES), :]
            rows_re, rows_im = [], []
            for k in range(SUBLANES):
                kb = SUBLANES - 1 - k
                rows_re.append(x_re)
                rows_im.append(x_im)
                s_re = jnp.where(fwd_lane, sf[k:k + 1, :LANES], sb[kb:kb + 1, :LANES])
                s_im = jnp.where(fwd_lane, sf[k:k + 1, LANES:], sb[kb:kb + 1, LANES:])
                x_re, x_im = a_re * x_re - a_im * x_im + s_re, a_re * x_im + a_im * x_re + s_im
            xf_ref[pl.ds(rf, SUBLANES), :LANES] = _rows_to_block(rows_re)
            xf_ref[pl.ds(rf, SUBLANES), LANES:] = _rows_to_block(rows_im)
            xb_ref[pl.ds(rb, SUBLANES), :LANES] = _rows_to_block(rows_re[::-1])
            xb_ref[pl.ds(rb, SUBLANES), LANES:] = _rows_to_block(rows_im[::-1])
            new += [x_re, x_im]
        return tuple(new)

    zero = jnp.zeros((1, LANES), F32)
    lax.fori_loop(0, n_blk, step, (zero,) * (2 * n_seq))
    y_ref[0] = (_dot(u, toep_ref[0]) + _dot(xf_ref[...].astype(BF16), w2_ref[0, 0])
                + _dot(xb_ref[...].astype(BF16), w2_ref[0, 1]))


def ssm(u_rows, toep, w1, w2, dec, n_seq, n_chunk):
    groups, rows, tp = u_rows.shape
    n_state = w1.shape[2] // 4
    assert 2 * n_state == LANES and n_chunk % SUBLANES == 0
    g3 = lambda g: (g, 0, 0)
    return pl.pallas_call(
        functools.partial(_ssm_kernel, n_seq=n_seq, n_chunk=n_chunk, n_state=n_state),
        grid=(groups,),
        in_specs=[pl.BlockSpec((1, rows, tp), g3), pl.BlockSpec((1, tp, tp), g3),
                  pl.BlockSpec((1, tp, 4 * n_state), g3), pl.BlockSpec((1, 2, 4 * n_state, tp), lambda g: (g, 0, 0, 0)),
                  pl.BlockSpec((1, 2, LANES), g3)],
        out_specs=pl.BlockSpec((1, rows, tp), g3),
        out_shape=jax.ShapeDtypeStruct((groups, rows, tp), F32),
        scratch_shapes=[pltpu.VMEM((rows, 4 * n_state), F32)] * 3,
        compiler_params=_params(1),
        name="ssm",
    )(u_rows, toep, w1, w2, dec)


def ssm_operators(a_re, a_im, log_step, b_re, b_im, c_re, c_im, d_skip, chunk):
    _, groups, n_state = a_re.shape
    p = b_re.shape[-1]
    step = jnp.exp(log_step)[..., None]
    mag = jnp.exp(step * a_re)
    ab_re, ab_im = mag * jnp.cos(step * a_im), mag * jnp.sin(step * a_im)
    den = a_re * a_re + a_im * a_im
    nr, ni = ab_re - 1.0, ab_im
    f_re = (nr * a_re + ni * a_im) / den
    f_im = (ni * a_re - nr * a_im) / den
    bb_re = f_re[..., None] * b_re - f_im[..., None] * b_im
    bb_im = f_re[..., None] * b_im + f_im[..., None] * b_re
    j = jnp.arange(chunk + 1, dtype=F32)[:, None, None, None]
    pw_mag = jnp.exp(j * (step * a_re)[None])
    pw_re = pw_mag * jnp.cos(j * (step * a_im)[None])
    pw_im = pw_mag * jnp.sin(j * (step * a_im)[None])
    ab_b_re = pw_re[..., None] * bb_re[None] - pw_im[..., None] * bb_im[None]
    ab_b_im = pw_re[..., None] * bb_im[None] + pw_im[..., None] * bb_re[None]
    taps = (jnp.einsum('dgqn,jdgnp->jdgqp', c_re, ab_b_re[:chunk])
            - jnp.einsum('dgqn,jdgnp->jdgqp', c_im, ab_b_im[:chunk]))
    s_idx = jnp.arange(chunk)[:, None]
    t_idx = jnp.arange(chunk)[None, :]
    lag = jnp.abs(t_idx - s_idx)
    kf = taps[lag, 0]
    kb = taps[lag, 1]
    diag = (taps[0, 0] + taps[0, 1] + d_skip.reshape(groups, p)[:, :, None] * jnp.eye(p, dtype=F32))
    blk = jnp.where((t_idx > s_idx)[..., None, None, None], kf,
                    jnp.where((t_idx < s_idx)[..., None, None, None], kb, diag[None, None]))
    toep = blk.transpose(2, 0, 4, 1, 3).reshape(groups, chunk * p, chunk * p)
    e_f = jnp.arange(chunk - 1, -1, -1)
    e_b = jnp.arange(chunk)
    w1 = jnp.stack([ab_b_re[e_f, 0], ab_b_re[e_b, 1], ab_b_im[e_f, 0], ab_b_im[e_b, 1]], axis=0)
    w1 = w1.transpose(2, 1, 4, 0, 3).reshape(groups, chunk * p, 4 * n_state)
    o_f = jnp.arange(1, chunk + 1)
    o_b = jnp.arange(chunk, 0, -1)
    cp_re = c_re[None] * pw_re[:, :, :, None, :] - c_im[None] * pw_im[:, :, :, None, :]
    cp_im = c_re[None] * pw_im[:, :, :, None, :] + c_im[None] * pw_re[:, :, :, None, :]
    zf = jnp.zeros_like(cp_re[o_f, 0])
    w2 = jnp.stack([jnp.stack([cp_re[o_f, 0], zf, -cp_im[o_f, 0], zf], axis=0),
                    jnp.stack([zf, cp_re[o_b, 1], zf, -cp_im[o_b, 1]], axis=0)], axis=0)
    w2 = w2.transpose(3, 0, 1, 5, 2, 4).reshape(groups, 2, 4 * n_state, chunk * p)
    dec_re = jnp.concatenate([pw_re[chunk, 0], pw_re[chunk, 1]], axis=-1)
    dec_im = jnp.concatenate([pw_im[chunk, 0], pw_im[chunk, 1]], axis=-1)
    dec = jnp.stack([dec_re, dec_im], axis=1)
    return toep.astype(BF16), w1.astype(BF16), w2.astype(BF16), dec


def _mix_kernel(xn_ref, o_ref, y_ref, wga_ref, wgs_ref, wo_ref, wz1_ref, wz2_ref, mix_ref, gy_ref):
    @pl.when(pl.program_id(1) == 0)
    def _():
        gy_ref[...] = jax.nn.gelu(y_ref[...]).astype(BF16)

    xn = xn_ref[...]
    gy = gy_ref[...]
    o_a = _dot(o_ref[...], wo_ref[...])
    o_s = _dot(gy, wz1_ref[...]) * jax.nn.sigmoid(_dot(gy, wz2_ref[...]))
    g_a = jax.nn.sigmoid(_dot(xn, wga_ref[...]))
    g_s = jax.nn.sigmoid(_dot(xn, wgs_ref[...]))
    mix_ref[...] = (g_a * o_a + g_s * o_s).astype(BF16)


def mix_merge(xn, o, y, wga, wgs, wo, wz1, wz2, tm, tn):
    t, d = xn.shape
    ho, sw = o.shape[1], y.shape[1]
    row = lambda i, j: (i, 0)
    col = lambda i, j: (0, j)
    return pl.pallas_call(
        _mix_kernel,
        grid=(t // tm, d // tn),
        in_specs=[pl.BlockSpec((tm, d), row), pl.BlockSpec((tm, ho), row), pl.BlockSpec((tm, sw), row),
                  pl.BlockSpec((d, tn), col), pl.BlockSpec((d, tn), col), pl.BlockSpec((ho, tn), col),
                  pl.BlockSpec((sw, tn), col), pl.BlockSpec((sw, tn), col)],
        out_specs=pl.BlockSpec((tm, tn), lambda i, j: (i, j)),
        out_shape=jax.ShapeDtypeStruct((t, d), BF16),
        scratch_shapes=[pltpu.VMEM((tm, sw), BF16)],
        compiler_params=_params(2, 1),
        name="mix_merge",
    )(xn, o, y, wga, wgs, wo, wz1, wz2)


def _resid_norm_kernel(x_ref, mix_ref, w_ref, g_ref, h_ref, n_ref):
    h = x_ref[...] + _dot(mix_ref[...], w_ref[...])
    h_ref[...] = h
    n_ref[...] = _rms(h, g_ref[...]).astype(BF16)


def resid_norm(x, mix, w, g, tm):
    t, d = x.shape
    row = lambda i: (i, 0)
    return pl.pallas_call(
        _resid_norm_kernel,
        grid=(t // tm,),
        in_specs=[pl.BlockSpec((tm, d), row), pl.BlockSpec((tm, d), row), _resident((d, d)), _resident((1, d))],
        out_specs=[pl.BlockSpec((tm, d), row), pl.BlockSpec((tm, d), row)],
        out_shape=[jax.ShapeDtypeStruct((t, d), F32), jax.ShapeDtypeStruct((t, d), BF16)],
        compiler_params=_params(1),
        name="resid_norm",
    )(x, mix, w, g.reshape(1, d))


def _mlp_kernel(n_ref, h_ref, w1_ref, w2_ref, o_ref, acc_ref):
    j = pl.program_id(1)

    @pl.when(j == 0)
    def _():
        acc_ref[...] = h_ref[...]

    a = jnp.maximum(_dot(n_ref[...], w1_ref[...]), 0.0)
    acc_ref[...] += _dot((a * a).astype(BF16), w2_ref[...])

    @pl.when(j == pl.num_programs(1) - 1)
    def _():
        o_ref[...] = acc_ref[...]


def mlp(n, h, w1, w2, tm, tf):
    t, d = h.shape
    ff = w1.shape[1]
    row = lambda i, j: (i, 0)
    return pl.pallas_call(
        _mlp_kernel,
        grid=(t // tm, ff // tf),
        in_specs=[pl.BlockSpec((tm, d), row), pl.BlockSpec((tm, d), row),
                  pl.BlockSpec((d, tf), lambda i, j: (0, j)), pl.BlockSpec((tf, d), lambda i, j: (j, 0))],
        out_specs=pl.BlockSpec((tm, d), row),
        out_shape=jax.ShapeDtypeStruct((t, d), F32),
        scratch_shapes=[pltpu.VMEM((tm, d), F32)],
        compiler_params=_params(2, 1),
        name="mlp",
    )(n, h, w1, w2)


def _ple_kernel(h_ref, p_ref, wg_ref, wp_ref, gp_ref, gf_ref, o_ref, *, final):
    h = h_ref[...]
    gate = jax.nn.sigmoid(_dot(_rms(h, gp_ref[...]).astype(BF16), wg_ref[...]))
    h = h + gate * _dot(p_ref[...].astype(BF16), wp_ref[...])
    o_ref[...] = _rms(h, gf_ref[...]) if final else h


def ple(h, p, wg, wp, gp, gf, tm, final):
    t, d = h.shape
    pd = p.shape[1]
    row = lambda i: (i, 0)
    return pl.pallas_call(
        functools.partial(_ple_kernel, final=final),
        grid=(t // tm,),
        in_specs=[pl.BlockSpec((tm, d), row), pl.BlockSpec((tm, pd), row), _resident((d, d)), _resident((pd, d)),
                  _resident((1, d)), _resident((1, d))],
        out_specs=pl.BlockSpec((tm, d), row),
        out_shape=jax.ShapeDtypeStruct((t, d), F32),
        compiler_params=_params(1),
        name="ple",
    )(h, p, wg, wp, gp.reshape(1, d), gf.reshape(1, d))


def _rope_tables(length, rope_dim):
    pos = jnp.arange(length, dtype=F32)
    inv = ROPE_THETA ** (-jnp.arange(0, rope_dim, 2, dtype=F32) / rope_dim)
    ang = pos[:, None] * inv[None, :]
    return jnp.cos(ang), jnp.sin(ang)


def _rope_lanes(cos, sin):
    length, half = cos.shape
    z = jnp.zeros((length, LANES // 2 - half), F32)
    return (jnp.concatenate([cos, z, cos, z], axis=1), jnp.concatenate([-sin, z, sin, z], axis=1))


def _spread_rope(w, half):
    z = jnp.zeros(w.shape[:-1] + (LANES // 2 - half,), w.dtype)
    return jnp.concatenate([w[..., :half], z, w[..., half:], z], axis=-1)


def _tile(n, pref):
    t = min(n, pref)
    while n % t:
        t //= 2
    return t


def _layer_weights(w_in, norm_q, w_uq, norm_kv, w_uk, w_uv, w_o_attn, ssm, w_glu, d_model):
    cq, ckv = norm_q.shape[0], norm_kv.shape[0]
    n_heads, dn = w_uk.shape[1], w_uk.shape[2]
    dv = w_uv.shape[2]
    sw = ssm[7].shape[0]
    rope = w_in.shape[1] - (cq + ckv + sw + 2 * d_model)
    half = rope // 2
    o_q, o_kv, o_kr, o_u, o_ga, o_gs = 0, cq, cq + ckv, cq + ckv + rope, cq + ckv + rope + sw, cq + ckv + rope + sw + d_model
    scale = (dn + rope) ** -0.5 * math.log2(math.e)
    wuq = w_uq.reshape(cq, n_heads, dn + rope) * scale
    wuq_t = jnp.concatenate([wuq[..., :dn], _spread_rope(wuq[..., dn:], half)], axis=-1).transpose(1, 2, 0)
    return dict(
        rope=rope, n_heads=n_heads, dn=dn, dv=dv,
        wq=w_in[:, o_q:o_kv].astype(BF16), wkv=w_in[:, o_kv:o_kr].astype(BF16),
        wkr=_spread_rope(w_in[:, o_kr:o_u], half).astype(BF16),
        wu=w_in[:, o_u:o_ga].astype(BF16), wga=w_in[:, o_ga:o_gs].astype(BF16), wgs=w_in[:, o_gs:].astype(BF16),
        wuq_t=wuq_t.astype(BF16),
        wuk=w_uk.reshape(ckv, n_heads * dn).astype(BF16),
        wuv_t=w_uv.reshape(ckv, n_heads * dv).T.astype(BF16),
        wo=w_o_attn.astype(BF16),
        wz1=w_glu[:, :d_model].astype(BF16), wz2=w_glu[:, d_model:].astype(BF16),
        ssm_ops=ssm_operators(*ssm, SSM_CHUNK),
    )


def _encoder_layer(h, p, lw, norm_mix, norm_q, norm_kv, w_out, norm_mlp, w_mlp_in, w_mlp_out,
                   norm_ple, w_ple_gate, w_ple, norm_out, final, batch, seq_len):
    t, d = h.shape
    tm = _tile(seq_len, 512)
    xn = norm_cast(h, norm_mix, tm)
    cos, sin = _rope_tables(seq_len, lw['rope'])
    cos_l, sin_l = _rope_lanes(cos, sin)
    qn, ckv, kr = qkv_proj(xn, lw['wq'], lw['wkv'], lw['wkr'], norm_q, norm_kv, cos_l, sin_l, seq_len, tm)
    qt, k, vt = attn_prep(qn, ckv, kr, lw['wuq_t'], lw['wuk'], lw['wuv_t'], cos_l.T, sin_l.T, batch, seq_len, tm)
    o = flash_attn(qt, k, vt, lw['dv'], _tile(seq_len, 512), _tile(seq_len, 512)).reshape(t, -1)
    toep, w1, w2, dec = lw['ssm_ops']
    groups = toep.shape[0]
    gp = lw['wu'].shape[1] // groups
    n_chunk = seq_len // SSM_CHUNK
    u = mm(xn, lw['wu'], tm, _tile(lw['wu'].shape[1], 1024), BF16)
    u_rows = (u.reshape(batch * n_chunk, SSM_CHUNK, groups, gp).transpose(2, 0, 1, 3)
              .reshape(groups, batch * n_chunk, SSM_CHUNK * gp))
    y_rows = ssm(u_rows, toep, w1, w2, dec, batch, n_chunk)
    y = (y_rows.reshape(groups, batch * n_chunk, SSM_CHUNK, gp).transpose(1, 2, 0, 3).reshape(t, groups * gp))
    mix = mix_merge(xn, o, y, lw['wga'], lw['wgs'], lw['wo'], lw['wz1'], lw['wz2'], tm, _tile(d, 512))
    h1, n1 = resid_norm(h, mix, w_out.astype(BF16), norm_mlp, _tile(seq_len, 256))
    h2 = mlp(n1, h1, w_mlp_in.astype(BF16), w_mlp_out.astype(BF16), tm, _tile(w_mlp_in.shape[1], 1024))
    return ple(h2, p, w_ple_gate.astype(BF16), w_ple.astype(BF16), norm_ple, norm_out, _tile(seq_len, 256), final)


def kernel(x_prompt, x_sample, p_prompt, p_sample, norm_mix, w_in, norm_q, w_uq, norm_kv, w_uk, w_uv, w_o_attn, ssm_a_re, ssm_a_im, ssm_log_step, ssm_b_re, ssm_b_im, ssm_c_re, ssm_c_im, ssm_d, w_glu, w_out, norm_mlp, w_mlp_in, w_mlp_out, norm_ple, w_ple_gate, w_ple, norm_final):
    depth = w_in.shape[0]
    d_model = x_prompt.shape[-1]
    layers = []
    for i in range(depth):
        ssm_w = (ssm_a_re[i], ssm_a_im[i], ssm_log_step[i], ssm_b_re[i], ssm_b_im[i], ssm_c_re[i], ssm_c_im[i], ssm_d[i])
        layers.append(_layer_weights(w_in[i], norm_q[i], w_uq[i], norm_kv[i], w_uk[i], w_uv[i], w_o_attn[i],
                                     ssm_w, w_glu[i], d_model))
    outs = []
    for x, p in ((x_prompt, p_prompt), (x_sample, p_sample)):
        batch, seq_len, _ = x.shape
        h = x.reshape(batch * seq_len, d_model)
        for i in range(depth):
            final = i == depth - 1
            h = _encoder_layer(h, p[i].reshape(batch * seq_len, -1), layers[i], norm_mix[i], norm_q[i], norm_kv[i],
                               w_out[i], norm_mlp[i], w_mlp_in[i], w_mlp_out[i], norm_ple[i], w_ple_gate[i], w_ple[i],
                               norm_final if final else norm_ple[i], final, batch, seq_len)
        outs.append(h.reshape(batch, seq_len, d_model))
    return tuple(outs)
```

```python
import functools
import math

import jax
import jax.numpy as jnp
from jax import lax
from jax.experimental import pallas as pl
from jax.experimental.pallas import tpu as pltpu

NORM_EPS = 1e-6
ROPE_THETA = 10000.0
LANES = 128
SUBLANES = 8
BF16_ROWS = 16
FLASH_TQ = 1024
FLASH_TK = 512
FLASH_UNROLL = 4
MXU_DIM = 256
VMEM_LIMIT = 56 * 1024 * 1024
SSM_CHUNK = 32
BF16 = jnp.bfloat16
F32 = jnp.float32


def _params(n_axes, n_parallel=None):
    n_parallel = n_axes if n_parallel is None else n_parallel
    sem = ("parallel",) * n_parallel + ("arbitrary",) * (n_axes - n_parallel)
    return pltpu.CompilerParams(dimension_semantics=sem, vmem_limit_bytes=VMEM_LIMIT)


def _resident(shape):
    nd = len(shape)
    return pl.BlockSpec(shape, lambda *_: (0,) * nd, pipeline_mode=pl.Buffered(1))


def _rms(x, g):
    r = lax.rsqrt(jnp.mean(x * x, axis=-1, keepdims=True) + NORM_EPS)
    return (x * r) * g


def _dot(a, b):
    return jnp.dot(a, b, preferred_element_type=F32)


def _dot_nt(a, b):
    return lax.dot_general(a, b, (((1,), (1,)), ((), ())), preferred_element_type=F32)


def _norm_cast_kernel(x_ref, g_ref, o_ref):
    o_ref[...] = _rms(x_ref[...], g_ref[...]).astype(BF16)


def norm_cast(x, g, tm):
    t, d = x.shape
    return pl.pallas_call(
        _norm_cast_kernel,
        grid=(t // tm,),
        in_specs=[pl.BlockSpec((tm, d), lambda i: (i, 0)), _resident((1, d))],
        out_specs=pl.BlockSpec((tm, d), lambda i: (i, 0)),
        out_shape=jax.ShapeDtypeStruct((t, d), BF16),
        compiler_params=_params(1),
        name="norm_cast",
    )(x, g.reshape(1, d))


def _mm_kernel(a_ref, w_ref, o_ref):
    o_ref[...] = _dot(a_ref[...], w_ref[...]).astype(o_ref.dtype)


def mm(a, w, tm, tn, out_dtype):
    m, k = a.shape
    n = w.shape[1]
    return pl.pallas_call(
        _mm_kernel,
        grid=(m // tm, n // tn),
        in_specs=[pl.BlockSpec((tm, k), lambda i, j: (i, 0)), pl.BlockSpec((k, tn), lambda i, j: (0, j))],
        out_specs=pl.BlockSpec((tm, tn), lambda i, j: (i, j)),
        out_shape=jax.ShapeDtypeStruct((m, n), out_dtype),
        compiler_params=_params(2),
        name="mm",
    )(a, w)


def _qkv_kernel(xn_ref, wq_ref, wkv_ref, wkr_ref, gq_ref, gkv_ref, cos_ref, sin_ref,
                qn_ref, ckv_ref, kr_ref):
    xn = xn_ref[...]
    qn_ref[...] = _rms(_dot(xn, wq_ref[...]), gq_ref[...]).astype(BF16)
    ckv_ref[...] = _rms(_dot(xn, wkv_ref[...]), gkv_ref[...]).astype(BF16)
    kr = _dot(xn, wkr_ref[...])
    kr_ref[...] = (kr * cos_ref[...] + pltpu.roll(kr, LANES // 2, 1) * sin_ref[...]).astype(BF16)


def qkv_proj(xn, wq, wkv, wkr, gq, gkv, cos_l, sin_l, seq_len, tm):
    t, d = xn.shape
    cq, ckv = wq.shape[1], wkv.shape[1]
    npos = seq_len // tm
    row = lambda i: (i, 0)
    pos = lambda i: (i % npos, 0)
    return pl.pallas_call(
        _qkv_kernel,
        grid=(t // tm,),
        in_specs=[pl.BlockSpec((tm, d), row), _resident((d, cq)), _resident((d, ckv)), _resident((d, LANES)),
                  _resident((1, cq)), _resident((1, ckv)),
                  pl.BlockSpec((tm, LANES), pos), pl.BlockSpec((tm, LANES), pos)],
        out_specs=[pl.BlockSpec((tm, cq), row), pl.BlockSpec((tm, ckv), row), pl.BlockSpec((tm, LANES), row)],
        out_shape=[jax.ShapeDtypeStruct((t, cq), BF16), jax.ShapeDtypeStruct((t, ckv), BF16),
                   jax.ShapeDtypeStruct((t, LANES), BF16)],
        compiler_params=_params(1),
        name="qkv_proj",
    )(xn, wq, wkv, wkr, gq.reshape(1, cq), gkv.reshape(1, ckv), cos_l, sin_l)


def _attn_prep_kernel(qn_ref, ckv_ref, kr_ref, wuq_ref, wuk_ref, wuv_ref, cos_ref, sin_ref,
                      qt_ref, k_ref, vt_ref, *, n_heads, dn, dv):
    qn = qn_ref[...]
    ckv = ckv_ref[...]
    cos_t = cos_ref[...]
    sin_t = sin_ref[...]
    half = LANES // 2
    for h in range(n_heads):
        q = _dot_nt(wuq_ref[h], qn)
        qr = q[dn:]
        qr_sw = jnp.concatenate([qr[half:], qr[:half]], axis=0)
        qt_ref[0, h, :dn, :] = q[:dn].astype(BF16)
        qt_ref[0, h, dn:, :] = (qr * cos_t + qr_sw * sin_t).astype(BF16)
    k_all = _dot(ckv, wuk_ref[...])
    kr = kr_ref[...]
    for h in range(n_heads):
        k_ref[0, h, :, :dn] = k_all[:, h * dn:(h + 1) * dn].astype(BF16)
        k_ref[0, h, :, dn:] = kr
    vt = _dot_nt(wuv_ref[...], ckv).astype(BF16)
    tm = vt.shape[1]
    ones_rows = (lax.broadcasted_iota(jnp.int32, (BF16_ROWS, tm), 0) == 0).astype(BF16)
    dve = dv + BF16_ROWS
    for h in range(n_heads):
        vt_ref[0, h * dve:h * dve + dv, :] = vt[h * dv:(h + 1) * dv]
        vt_ref[0, h * dve + dv:(h + 1) * dve, :] = ones_rows


def attn_prep(qn, ckv, kr, wuq_t, wuk, wuv_t, cos_t, sin_t, batch, seq_len, tm):
    n_heads, dk, cq = wuq_t.shape
    dn = dk - LANES
    dv = wuv_t.shape[0] // n_heads
    hdv = n_heads * (dv + BF16_ROWS)
    ckv_dim = ckv.shape[1]
    npos = seq_len // tm
    row = lambda b, i: (b * npos + i, 0)
    col = lambda b, i: (0, i)
    return pl.pallas_call(
        functools.partial(_attn_prep_kernel, n_heads=n_heads, dn=dn, dv=dv),
        grid=(batch, npos),
        in_specs=[pl.BlockSpec((tm, cq), row), pl.BlockSpec((tm, ckv_dim), row), pl.BlockSpec((tm, LANES), row),
                  _resident((n_heads, dk, cq)), _resident((ckv_dim, n_heads * dn)), _resident(wuv_t.shape),
                  pl.BlockSpec((LANES, tm), col), pl.BlockSpec((LANES, tm), col)],
        out_specs=[pl.BlockSpec((1, n_heads, dk, tm), lambda b, i: (b, 0, 0, i)),
                   pl.BlockSpec((1, n_heads, tm, dk), lambda b, i: (b, 0, i, 0)),
                   pl.BlockSpec((1, hdv, tm), lambda b, i: (b, 0, i))],
        out_shape=[jax.ShapeDtypeStruct((batch, n_heads, dk, seq_len), BF16),
                   jax.ShapeDtypeStruct((batch, n_heads, seq_len, dk), BF16),
                   jax.ShapeDtypeStruct((batch, hdv, seq_len), BF16)],
        compiler_params=_params(2),
        name="attn_prep",
    )(qn, ckv, kr, wuq_t, wuk, wuv_t, cos_t, sin_t)


def _flash_kernel(qt_ref, k_ref, vt_ref, o_ref, m_ref, acc_ref, s0_ref, s1_ref, *, tk, nk, dv, unroll):
    qt = qt_ref[0, 0]
    m_ref[...] = jnp.full(m_ref.shape, -jnp.inf, F32)
    acc_ref[...] = jnp.zeros(acc_ref.shape, F32)

    def scores(j, s_ref):
        off = pl.multiple_of(j * tk, tk)
        s_ref[...] = _dot(k_ref[0, 0, pl.ds(off, tk), :], qt)

    def consume(j, s_ref):
        off = pl.multiple_of(j * tk, tk)
        s = s_ref[...]
        m_prev = m_ref[...]
        m_new = jnp.maximum(m_prev, jnp.max(s, axis=0, keepdims=True))
        alpha = jnp.exp2(m_prev - m_new)
        p = jnp.exp2(s - m_new).astype(BF16)
        acc_ref[...] = alpha * acc_ref[...] + _dot(vt_ref[0, :, pl.ds(off, tk)], p)
        m_ref[...] = m_new

    bufs = (s0_ref, s1_ref)
    scores(0, s0_ref)

    def group(base, last):
        for u in range(unroll):
            if not (last and u == unroll - 1):
                scores(base + u + 1, bufs[(u + 1) % 2])
            consume(base + u, bufs[u % 2])

    def body(i, carry):
        group(i * unroll, False)
        return carry

    lax.fori_loop(0, nk // unroll - 1, body, 0)
    group(nk - unroll, True)
    acc = acc_ref[...]
    o_ref[0] = (acc[:dv] / acc[dv:dv + 1]).T.astype(BF16)


def flash_attn(qt, k, vt, dv, tq, tk):
    batch, n_heads, dk, seq_len = qt.shape
    dve = vt.shape[1] // n_heads
    nk = seq_len // tk
    unroll = min(FLASH_UNROLL, nk)
    assert unroll % 2 == 0 and nk % unroll == 0
    return pl.pallas_call(
        functools.partial(_flash_kernel, tk=tk, nk=nk, dv=dv, unroll=unroll),
        grid=(batch, n_heads, seq_len // tq),
        in_specs=[pl.BlockSpec((1, 1, dk, tq), lambda b, h, i: (b, h, 0, i)),
                  pl.BlockSpec((1, 1, seq_len, dk), lambda b, h, i: (b, h, 0, 0)),
                  pl.BlockSpec((1, dve, seq_len), lambda b, h, i: (b, h, 0))],
        out_specs=pl.BlockSpec((1, tq, dv), lambda b, h, i: (b, i, h)),
        out_shape=jax.ShapeDtypeStruct((batch, seq_len, n_heads * dv), BF16),
        scratch_shapes=[pltpu.VMEM((1, tq), F32), pltpu.VMEM((dve, tq), F32),
                        pltpu.VMEM((tk, tq), F32), pltpu.VMEM((tk, tq), F32)],
        compiler_params=_params(3),
        name="flash_attn",
    )(qt, k, vt)


def _rows_to_block(rows):
    n = rows[0].shape[1]
    rid = lax.broadcasted_iota(jnp.int32, (SUBLANES, n), 0)
    blk = jnp.broadcast_to(rows[-1], (SUBLANES, n))
    for k in range(SUBLANES - 2, -1, -1):
        blk = jnp.where(rid == k, rows[k], blk)
    return blk


def _ssm_kernel(u_ref, toep_ref, w1_ref, w2_ref, dec_ref, y_ref, s_ref, xf_ref, xb_ref, *, n_seq, n_chunk, n_state):
    u = u_ref[0]
    s_ref[...] = _dot(u, w1_ref[0])
    dec = dec_ref[0]
    a_re, a_im = dec[0:1], dec[1:2]
    fwd_lane = lax.broadcasted_iota(jnp.int32, (1, LANES), 1) < n_state
    n_blk = n_chunk // SUBLANES

    def step(i, carry):
        new = []
        for q in range(n_seq):
            x_re, x_im = carry[2 * q], carry[2 * q + 1]
            rf = pl.multiple_of(q * n_chunk + i * SUBLANES, SUBLANES)
            rb = pl.multiple_of(q * n_chunk + (n_blk - 1 - i) * SUBLANES, SUBLANES)
            sf = s_ref[pl.ds(rf, SUBLANES), :]
            sb = s_ref[pl.ds(rb, SUBLANES), :]
            rows_re, rows_im = [], []
            for k in range(SUBLANES):
                kb = SUBLANES - 1 - k
                rows_re.append(x_re)
                rows_im.append(x_im)
                s_re = jnp.where(fwd_lane, sf[k:k + 1, :LANES], sb[kb:kb + 1, :LANES])
                s_im = jnp.where(fwd_lane, sf[k:k + 1, LANES:], sb[kb:kb + 1, LANES:])
                x_re, x_im = a_re * x_re - a_im * x_im + s_re, a_re * x_im + a_im * x_re + s_im
            xf_ref[pl.ds(rf, SUBLANES), :LANES] = _rows_to_block(rows_re)
            xf_ref[pl.ds(rf, SUBLANES), LANES:] = _rows_to_block(rows_im)
            xb_ref[pl.ds(rb, SUBLANES), :LANES] = _rows_to_block(rows_re[::-1])
            xb_ref[pl.ds(rb, SUBLANES), LANES:] = _rows_to_block(rows_im[::-1])
            new += [x_re, x_im]
        return tuple(new)

    zero = jnp.zeros((1, LANES), F32)
    lax.fori_loop(0, n_blk, step, (zero,) * (2 * n_seq))
    y_ref[0] = (_dot(u, toep_ref[0]) + _dot(xf_ref[...].astype(BF16), w2_ref[0, 0])
                + _dot(xb_ref[...].astype(BF16), w2_ref[0, 1]))


def ssm(u_rows, toep, w1, w2, dec, n_seq, n_chunk):
    groups, rows, tp = u_rows.shape
    n_state = w1.shape[2] // 4
    assert 2 * n_state == LANES and n_chunk % SUBLANES == 0
    g3 = lambda g: (g, 0, 0)
    return pl.pallas_call(
        functools.partial(_ssm_kernel, n_seq=n_seq, n_chunk=n_chunk, n_state=n_state),
        grid=(groups,),
        in_specs=[pl.BlockSpec((1, rows, tp), g3), pl.BlockSpec((1, tp, tp), g3),
                  pl.BlockSpec((1, tp, 4 * n_state), g3), pl.BlockSpec((1, 2, 4 * n_state, tp), lambda g: (g, 0, 0, 0)),
                  pl.BlockSpec((1, 2, LANES), g3)],
        out_specs=pl.BlockSpec((1, rows, tp), g3),
        out_shape=jax.ShapeDtypeStruct((groups, rows, tp), F32),
        scratch_shapes=[pltpu.VMEM((rows, 4 * n_state), F32)] * 3,
        compiler_params=_params(1),
        name="ssm",
    )(u_rows, toep, w1, w2, dec)


def ssm_operators(a_re, a_im, log_step, b_re, b_im, c_re, c_im, d_skip, chunk):
    _, groups, n_state = a_re.shape
    p = b_re.shape[-1]
    step = jnp.exp(log_step)[..., None]
    mag = jnp.exp(step * a_re)
    ab_re, ab_im = mag * jnp.cos(step * a_im), mag * jnp.sin(step * a_im)
    den = a_re * a_re + a_im * a_im
    nr, ni = ab_re - 1.0, ab_im
    f_re = (nr * a_re + ni * a_im) / den
    f_im = (ni * a_re - nr * a_im) / den
    bb_re = f_re[..., None] * b_re - f_im[..., None] * b_im
    bb_im = f_re[..., None] * b_im + f_im[..., None] * b_re
    j = jnp.arange(chunk + 1, dtype=F32)[:, None, None, None]
    pw_mag = jnp.exp(j * (step * a_re)[None])
    pw_re = pw_mag * jnp.cos(j * (step * a_im)[None])
    pw_im = pw_mag * jnp.sin(j * (step * a_im)[None])
    ab_b_re = pw_re[..., None] * bb_re[None] - pw_im[..., None] * bb_im[None]
    ab_b_im = pw_re[..., None] * bb_im[None] + pw_im[..., None] * bb_re[None]
    taps = (jnp.einsum('dgqn,jdgnp->jdgqp', c_re, ab_b_re[:chunk])
            - jnp.einsum('dgqn,jdgnp->jdgqp', c_im, ab_b_im[:chunk]))
    s_idx = jnp.arange(chunk)[:, None]
    t_idx = jnp.arange(chunk)[None, :]
    lag = jnp.abs(t_idx - s_idx)
    kf = taps[lag, 0]
    kb = taps[lag, 1]
    diag = (taps[0, 0] + taps[0, 1] + d_skip.reshape(groups, p)[:, :, None] * jnp.eye(p, dtype=F32))
    blk = jnp.where((t_idx > s_idx)[..., None, None, None], kf,
                    jnp.where((t_idx < s_idx)[..., None, None, None], kb, diag[None, None]))
    toep = blk.transpose(2, 0, 4, 1, 3).reshape(groups, chunk * p, chunk * p)
    e_f = jnp.arange(chunk - 1, -1, -1)
    e_b = jnp.arange(chunk)
    w1 = jnp.stack([ab_b_re[e_f, 0], ab_b_re[e_b, 1], ab_b_im[e_f, 0], ab_b_im[e_b, 1]], axis=0)
    w1 = w1.transpose(2, 1, 4, 0, 3).reshape(groups, chunk * p, 4 * n_state)
    o_f = jnp.arange(1, chunk + 1)
    o_b = jnp.arange(chunk, 0, -1)
    cp_re = c_re[None] * pw_re[:, :, :, None, :] - c_im[None] * pw_im[:, :, :, None, :]
    cp_im = c_re[None] * pw_im[:, :, :, None, :] + c_im[None] * pw_re[:, :, :, None, :]
    zf = jnp.zeros_like(cp_re[o_f, 0])
    w2 = jnp.stack([jnp.stack([cp_re[o_f, 0], zf, -cp_im[o_f, 0], zf], axis=0),
                    jnp.stack([zf, cp_re[o_b, 1], zf, -cp_im[o_b, 1]], axis=0)], axis=0)
    w2 = w2.transpose(3, 0, 1, 5, 2, 4).reshape(groups, 2, 4 * n_state, chunk * p)
    dec_re = jnp.concatenate([pw_re[chunk, 0], pw_re[chunk, 1]], axis=-1)
    dec_im = jnp.concatenate([pw_im[chunk, 0], pw_im[chunk, 1]], axis=-1)
    dec = jnp.stack([dec_re, dec_im], axis=1)
    return toep.astype(BF16), w1.astype(BF16), w2.astype(BF16), dec


def _mix_kernel(xn_ref, o_ref, y_ref, wga_ref, wgs_ref, wo_ref, wz1_ref, wz2_ref, mix_ref, gy_ref):
    @pl.when(pl.program_id(1) == 0)
    def _():
        gy_ref[...] = jax.nn.gelu(y_ref[...]).astype(BF16)

    xn = xn_ref[...]
    gy = gy_ref[...]
    o_a = _dot(o_ref[...], wo_ref[...])
    o_s = _dot(gy, wz1_ref[...]) * jax.nn.sigmoid(_dot(gy, wz2_ref[...]))
    g_a = jax.nn.sigmoid(_dot(xn, wga_ref[...]))
    g_s = jax.nn.sigmoid(_dot(xn, wgs_ref[...]))
    mix_ref[...] = (g_a * o_a + g_s * o_s).astype(BF16)


def mix_merge(xn, o, y, wga, wgs, wo, wz1, wz2, tm, tn):
    t, d = xn.shape
    ho, sw = o.shape[1], y.shape[1]
    row = lambda i, j: (i, 0)
    col = lambda i, j: (0, j)
    return pl.pallas_call(
        _mix_kernel,
        grid=(t // tm, d // tn),
        in_specs=[pl.BlockSpec((tm, d), row), pl.BlockSpec((tm, ho), row), pl.BlockSpec((tm, sw), row),
                  pl.BlockSpec((d, tn), col), pl.BlockSpec((d, tn), col), pl.BlockSpec((ho, tn), col),
                  pl.BlockSpec((sw, tn), col), pl.BlockSpec((sw, tn), col)],
        out_specs=pl.BlockSpec((tm, tn), lambda i, j: (i, j)),
        out_shape=jax.ShapeDtypeStruct((t, d), BF16),
        scratch_shapes=[pltpu.VMEM((tm, sw), BF16)],
        compiler_params=_params(2, 1),
        name="mix_merge",
    )(xn, o, y, wga, wgs, wo, wz1, wz2)


def _resid_norm_kernel(x_ref, mix_ref, w_ref, g_ref, h_ref, n_ref):
    h = x_ref[...] + _dot(mix_ref[...], w_ref[...])
    h_ref[...] = h
    n_ref[...] = _rms(h, g_ref[...]).astype(BF16)


def resid_norm(x, mix, w, g, tm):
    t, d = x.shape
    row = lambda i: (i, 0)
    return pl.pallas_call(
        _resid_norm_kernel,
        grid=(t // tm,),
        in_specs=[pl.BlockSpec((tm, d), row), pl.BlockSpec((tm, d), row), _resident((d, d)), _resident((1, d))],
        out_specs=[pl.BlockSpec((tm, d), row), pl.BlockSpec((tm, d), row)],
        out_shape=[jax.ShapeDtypeStruct((t, d), F32), jax.ShapeDtypeStruct((t, d), BF16)],
        compiler_params=_params(1),
        name="resid_norm",
    )(x, mix, w, g.reshape(1, d))


def _mlp_kernel(n_ref, h_ref, w1_ref, w2_ref, o_ref, acc_ref):
    j = pl.program_id(1)

    @pl.when(j == 0)
    def _():
        acc_ref[...] = h_ref[...]

    a = jnp.maximum(_dot(n_ref[...], w1_ref[...]), 0.0)
    acc_ref[...] += _dot((a * a).astype(BF16), w2_ref[...])

    @pl.when(j == pl.num_programs(1) - 1)
    def _():
        o_ref[...] = acc_ref[...]


def mlp(n, h, w1, w2, tm, tf):
    t, d = h.shape
    ff = w1.shape[1]
    row = lambda i, j: (i, 0)
    return pl.pallas_call(
        _mlp_kernel,
        grid=(t // tm, ff // tf),
        in_specs=[pl.BlockSpec((tm, d), row), pl.BlockSpec((tm, d), row),
                  pl.BlockSpec((d, tf), lambda i, j: (0, j)), pl.BlockSpec((tf, d), lambda i, j: (j, 0))],
        out_specs=pl.BlockSpec((tm, d), row),
        out_shape=jax.ShapeDtypeStruct((t, d), F32),
        scratch_shapes=[pltpu.VMEM((tm, d), F32)],
        compiler_params=_params(2, 1),
        name="mlp",
    )(n, h, w1, w2)


def _ple_kernel(h_ref, p_ref, wg_ref, wp_ref, gp_ref, gf_ref, o_ref, *, final):
    h = h_ref[...]
    gate = jax.nn.sigmoid(_dot(_rms(h, gp_ref[...]).astype(BF16), wg_ref[...]))
    h = h + gate * _dot(p_ref[...].astype(BF16), wp_ref[...])
    o_ref[...] = _rms(h, gf_ref[...]) if final else h


def ple(h, p, wg, wp, gp, gf, tm, final):
    t, d = h.shape
    pd = p.shape[1]
    row = lambda i: (i, 0)
    return pl.pallas_call(
        functools.partial(_ple_kernel, final=final),
        grid=(t // tm,),
        in_specs=[pl.BlockSpec((tm, d), row), pl.BlockSpec((tm, pd), row), _resident((d, d)), _resident((pd, d)),
                  _resident((1, d)), _resident((1, d))],
        out_specs=pl.BlockSpec((tm, d), row),
        out_shape=jax.ShapeDtypeStruct((t, d), F32),
        compiler_params=_params(1),
        name="ple",
    )(h, p, wg, wp, gp.reshape(1, d), gf.reshape(1, d))


def _rope_tables(length, rope_dim):
    pos = jnp.arange(length, dtype=F32)
    inv = ROPE_THETA ** (-jnp.arange(0, rope_dim, 2, dtype=F32) / rope_dim)
    ang = pos[:, None] * inv[None, :]
    return jnp.cos(ang), jnp.sin(ang)


def _rope_lanes(cos, sin):
    length, half = cos.shape
    z = jnp.zeros((length, LANES // 2 - half), F32)
    return (jnp.concatenate([cos, z, cos, z], axis=1), jnp.concatenate([-sin, z, sin, z], axis=1))


def _spread_rope(w, half):
    z = jnp.zeros(w.shape[:-1] + (LANES // 2 - half,), w.dtype)
    return jnp.concatenate([w[..., :half], z, w[..., half:], z], axis=-1)


def _tile(n, pref):
    t = min(n, pref)
    while n % t:
        t //= 2
    return t


def _layer_weights(w_in, norm_q, w_uq, norm_kv, w_uk, w_uv, w_o_attn, ssm, w_glu, d_model):
    cq, ckv = norm_q.shape[0], norm_kv.shape[0]
    n_heads, dn = w_uk.shape[1], w_uk.shape[2]
    dv = w_uv.shape[2]
    sw = ssm[7].shape[0]
    rope = w_in.shape[1] - (cq + ckv + sw + 2 * d_model)
    half = rope // 2
    o_q, o_kv, o_kr, o_u, o_ga, o_gs = 0, cq, cq + ckv, cq + ckv + rope, cq + ckv + rope + sw, cq + ckv + rope + sw + d_model
    scale = (dn + rope) ** -0.5 * math.log2(math.e)
    wuq = w_uq.reshape(cq, n_heads, dn + rope) * scale
    wuq_t = jnp.concatenate([wuq[..., :dn], _spread_rope(wuq[..., dn:], half)], axis=-1).transpose(1, 2, 0)
    return dict(
        rope=rope, n_heads=n_heads, dn=dn, dv=dv,
        wq=w_in[:, o_q:o_kv].astype(BF16), wkv=w_in[:, o_kv:o_kr].astype(BF16),
        wkr=_spread_rope(w_in[:, o_kr:o_u], half).astype(BF16),
        wu=w_in[:, o_u:o_ga].astype(BF16), wga=w_in[:, o_ga:o_gs].astype(BF16), wgs=w_in[:, o_gs:].astype(BF16),
        wuq_t=wuq_t.astype(BF16),
        wuk=w_uk.reshape(ckv, n_heads * dn).astype(BF16),
        wuv_t=w_uv.reshape(ckv, n_heads * dv).T.astype(BF16),
        wo=w_o_attn.astype(BF16),
        wz1=w_glu[:, :d_model].astype(BF16), wz2=w_glu[:, d_model:].astype(BF16),
        ssm_ops=ssm_operators(*ssm, SSM_CHUNK),
    )


def _encoder_layer(h, p, lw, norm_mix, norm_q, norm_kv, w_out, norm_mlp, w_mlp_in, w_mlp_out,
                   norm_ple, w_ple_gate, w_ple, norm_out, final, batch, seq_len):
    t, d = h.shape
    tm = _tile(seq_len, 512)
    xn = norm_cast(h, norm_mix, tm)
    cos, sin = _rope_tables(seq_len, lw['rope'])
    cos_l, sin_l = _rope_lanes(cos, sin)
    qn, ckv, kr = qkv_proj(xn, lw['wq'], lw['wkv'], lw['wkr'], norm_q, norm_kv, cos_l, sin_l, seq_len, tm)
    qt, k, vt = attn_prep(qn, ckv, kr, lw['wuq_t'], lw['wuk'], lw['wuv_t'], cos_l.T, sin_l.T, batch, seq_len, tm)
    o = flash_attn(qt, k, vt, lw['dv'], _tile(seq_len, FLASH_TQ), _tile(seq_len, FLASH_TK)).reshape(t, -1)
    toep, w1, w2, dec = lw['ssm_ops']
    groups = toep.shape[0]
    gp = lw['wu'].shape[1] // groups
    n_chunk = seq_len // SSM_CHUNK
    u = mm(xn, lw['wu'], tm, _tile(lw['wu'].shape[1], 1024), BF16)
    u_rows = (u.reshape(batch * n_chunk, SSM_CHUNK, groups, gp).transpose(2, 0, 1, 3)
              .reshape(groups, batch * n_chunk, SSM_CHUNK * gp))
    y_rows = ssm(u_rows, toep, w1, w2, dec, batch, n_chunk)
    y = (y_rows.reshape(groups, batch * n_chunk, SSM_CHUNK, gp).transpose(1, 2, 0, 3).reshape(t, groups * gp))
    mix = mix_merge(xn, o, y, lw['wga'], lw['wgs'], lw['wo'], lw['wz1'], lw['wz2'], tm, _tile(d, 512))
    h1, n1 = resid_norm(h, mix, w_out.astype(BF16), norm_mlp, _tile(seq_len, 256))
    h2 = mlp(n1, h1, w_mlp_in.astype(BF16), w_mlp_out.astype(BF16), tm, _tile(w_mlp_in.shape[1], 1024))
    return ple(h2, p, w_ple_gate.astype(BF16), w_ple.astype(BF16), norm_ple, norm_out, _tile(seq_len, 256), final)


def kernel(x_prompt, x_sample, p_prompt, p_sample, norm_mix, w_in, norm_q, w_uq, norm_kv, w_uk, w_uv, w_o_attn, ssm_a_re, ssm_a_im, ssm_log_step, ssm_b_re, ssm_b_im, ssm_c_re, ssm_c_im, ssm_d, w_glu, w_out, norm_mlp, w_mlp_in, w_mlp_out, norm_ple, w_ple_gate, w_ple, norm_final):
    depth = w_in.shape[0]
    d_model = x_prompt.shape[-1]
    layers = []
    for i in range(depth):
        ssm_w = (ssm_a_re[i], ssm_a_im[i], ssm_log_step[i], ssm_b_re[i], ssm_b_im[i], ssm_c_re[i], ssm_c_im[i], ssm_d[i])
        layers.append(_layer_weights(w_in[i], norm_q[i], w_uq[i], norm_kv[i], w_uk[i], w_uv[i], w_o_attn[i],
                                     ssm_w, w_glu[i], d_model))
    outs = []
    for x, p in ((x_prompt, p_prompt), (x_sample, p_sample)):
        batch, seq_len, _ = x.shape
        h = x.reshape(batch * seq_len, d_model)
        for i in range(depth):
            final = i == depth - 1
            h = _encoder_layer(h, p[i].reshape(batch * seq_len, -1), layers[i], norm_mix[i], norm_q[i], norm_kv[i],
                               w_out[i], norm_mlp[i], w_mlp_in[i], w_mlp_out[i], norm_ple[i], w_ple_gate[i], w_ple[i],
                               norm_final if final else norm_ple[i], final, batch, seq_len)
        outs.append(h.reshape(batch, seq_len, d_model))
    return tuple(outs)
```

```python
import functools
import math

import jax
import jax.numpy as jnp
from jax import lax
from jax.experimental import pallas as pl
from jax.experimental.pallas import tpu as pltpu

NORM_EPS = 1e-6
ROPE_THETA = 10000.0
LANES = 128
SUBLANES = 8
BF16_ROWS = 16
FLASH_TQ = 1024
FLASH_TK = 512
FLASH_UNROLL = 8
MXU_DIM = 256
VMEM_LIMIT = 56 * 1024 * 1024
SSM_CHUNK = 32
BF16 = jnp.bfloat16
F32 = jnp.float32


def _params(n_axes, n_parallel=None):
    n_parallel = n_axes if n_parallel is None else n_parallel
    sem = ("parallel",) * n_parallel + ("arbitrary",) * (n_axes - n_parallel)
    return pltpu.CompilerParams(dimension_semantics=sem, vmem_limit_bytes=VMEM_LIMIT)


def _resident(shape):
    nd = len(shape)
    return pl.BlockSpec(shape, lambda *_: (0,) * nd, pipeline_mode=pl.Buffered(1))


def _rms(x, g):
    r = lax.rsqrt(jnp.mean(x * x, axis=-1, keepdims=True) + NORM_EPS)
    return (x * r) * g


def _dot(a, b):
    return jnp.dot(a, b, preferred_element_type=F32)


def _dot_nt(a, b):
    return lax.dot_general(a, b, (((1,), (1,)), ((), ())), preferred_element_type=F32)


def _in_proj_kernel(x_ref, g_ref, wq_ref, wkv_ref, wkr_ref, wu_ref, gq_ref, gkv_ref, cos_ref, sin_ref,
                    xn_ref, qn_ref, ckv_ref, kr_ref, u_ref):
    xn = _rms(x_ref[...], g_ref[...]).astype(BF16)
    xn_ref[...] = xn
    qn_ref[...] = _rms(_dot(xn, wq_ref[...]), gq_ref[...]).astype(BF16)
    ckv_ref[...] = _rms(_dot(xn, wkv_ref[...]), gkv_ref[...]).astype(BF16)
    kr = _dot(xn, wkr_ref[...])
    kr_ref[...] = (kr * cos_ref[...] + pltpu.roll(kr, LANES // 2, 1) * sin_ref[...]).astype(BF16)
    u_ref[...] = _dot(xn, wu_ref[...]).astype(BF16)


def in_proj(x, g, wq, wkv, wkr, wu, gq, gkv, cos_l, sin_l, seq_len, tm):
    t, d = x.shape
    cq, ckv, sw = wq.shape[1], wkv.shape[1], wu.shape[1]
    npos = seq_len // tm
    row = lambda i: (i, 0)
    pos = lambda i: (i % npos, 0)
    widths = (d, cq, ckv, LANES, sw)
    return pl.pallas_call(
        _in_proj_kernel,
        grid=(t // tm,),
        in_specs=[pl.BlockSpec((tm, d), row), _resident((1, d)),
                  _resident((d, cq)), _resident((d, ckv)), _resident((d, LANES)), _resident((d, sw)),
                  _resident((1, cq)), _resident((1, ckv)),
                  pl.BlockSpec((tm, LANES), pos), pl.BlockSpec((tm, LANES), pos)],
        out_specs=[pl.BlockSpec((tm, w), row) for w in widths],
        out_shape=[jax.ShapeDtypeStruct((t, w), BF16) for w in widths],
        compiler_params=_params(1),
        name="in_proj",
    )(x, g.reshape(1, d), wq, wkv, wkr, wu, gq.reshape(1, cq), gkv.reshape(1, ckv), cos_l, sin_l)


def _attn_prep_kernel(qn_ref, ckv_ref, kr_ref, wuq_ref, wuk_ref, wuv_ref, cos_ref, sin_ref,
                      qt_ref, k_ref, vt_ref, *, n_heads, dn, dv):
    qn = qn_ref[...]
    ckv = ckv_ref[...]
    cos_t = cos_ref[...]
    sin_t = sin_ref[...]
    half = LANES // 2
    for h in range(n_heads):
        q = _dot_nt(wuq_ref[h], qn)
        qr = q[dn:]
        qr_sw = jnp.concatenate([qr[half:], qr[:half]], axis=0)
        qt_ref[0, h, :dn, :] = q[:dn].astype(BF16)
        qt_ref[0, h, dn:, :] = (qr * cos_t + qr_sw * sin_t).astype(BF16)
    k_all = _dot(ckv, wuk_ref[...])
    kr = kr_ref[...]
    for h in range(n_heads):
        k_ref[0, h, :, :dn] = k_all[:, h * dn:(h + 1) * dn].astype(BF16)
        k_ref[0, h, :, dn:] = kr
    vt = _dot_nt(wuv_ref[...], ckv).astype(BF16)
    tm = vt.shape[1]
    ones_rows = (lax.broadcasted_iota(jnp.int32, (BF16_ROWS, tm), 0) == 0).astype(BF16)
    dve = dv + BF16_ROWS
    for h in range(n_heads):
        vt_ref[0, h * dve:h * dve + dv, :] = vt[h * dv:(h + 1) * dv]
        vt_ref[0, h * dve + dv:(h + 1) * dve, :] = ones_rows


def attn_prep(qn, ckv, kr, wuq_t, wuk, wuv_t, cos_t, sin_t, batch, seq_len, tm):
    n_heads, dk, cq = wuq_t.shape
    dn = dk - LANES
    dv = wuv_t.shape[0] // n_heads
    hdv = n_heads * (dv + BF16_ROWS)
    ckv_dim = ckv.shape[1]
    npos = seq_len // tm
    row = lambda b, i: (b * npos + i, 0)
    col = lambda b, i: (0, i)
    return pl.pallas_call(
        functools.partial(_attn_prep_kernel, n_heads=n_heads, dn=dn, dv=dv),
        grid=(batch, npos),
        in_specs=[pl.BlockSpec((tm, cq), row), pl.BlockSpec((tm, ckv_dim), row), pl.BlockSpec((tm, LANES), row),
                  _resident((n_heads, dk, cq)), _resident((ckv_dim, n_heads * dn)), _resident(wuv_t.shape),
                  pl.BlockSpec((LANES, tm), col), pl.BlockSpec((LANES, tm), col)],
        out_specs=[pl.BlockSpec((1, n_heads, dk, tm), lambda b, i: (b, 0, 0, i)),
                   pl.BlockSpec((1, n_heads, tm, dk), lambda b, i: (b, 0, i, 0)),
                   pl.BlockSpec((1, hdv, tm), lambda b, i: (b, 0, i))],
        out_shape=[jax.ShapeDtypeStruct((batch, n_heads, dk, seq_len), BF16),
                   jax.ShapeDtypeStruct((batch, n_heads, seq_len, dk), BF16),
                   jax.ShapeDtypeStruct((batch, hdv, seq_len), BF16)],
        compiler_params=_params(2),
        name="attn_prep",
    )(qn, ckv, kr, wuq_t, wuk, wuv_t, cos_t, sin_t)


def _flash_kernel(qt_ref, k_ref, vt_ref, o_ref, m_ref, acc_ref, s0_ref, s1_ref, c0_ref, c1_ref,
                  *, tk, nk, dv, unroll):
    qt = qt_ref[0, 0]
    tq = qt.shape[1]
    m_ref[...] = jnp.full(m_ref.shape, -jnp.inf, F32)
    acc_ref[...] = jnp.zeros(acc_ref.shape, F32)

    def scores(j, s_ref, c_ref):
        off = pl.multiple_of(j * tk, tk)
        k = k_ref[0, 0, pl.ds(off, tk), :]
        for c in range(0, tq, MXU_DIM):
            s = _dot(k, qt[:, c:c + MXU_DIM])
            s_ref[:, c:c + MXU_DIM] = s
            c_ref[:, c:c + MXU_DIM] = jnp.max(s, axis=0, keepdims=True)

    def consume(j, s_ref, c_ref):
        off = pl.multiple_of(j * tk, tk)
        vt = vt_ref[0, :, pl.ds(off, tk)]
        for c in range(0, tq, MXU_DIM):
            cols = slice(c, c + MXU_DIM)
            m_prev = m_ref[:, cols]
            m_new = jnp.maximum(m_prev, c_ref[:, cols])
            alpha = jnp.exp2(m_prev - m_new)
            p = jnp.exp2(s_ref[:, cols] - m_new).astype(BF16)
            acc_ref[:, cols] = alpha * acc_ref[:, cols] + _dot(vt, p)
            m_ref[:, cols] = m_new

    bufs = ((s0_ref, c0_ref), (s1_ref, c1_ref))
    scores(0, *bufs[0])

    def group(base, last):
        for u in range(unroll):
            if not (last and u == unroll - 1):
                scores(base + u + 1, *bufs[(u + 1) % 2])
            consume(base + u, *bufs[u % 2])

    def body(i, carry):
        group(i * unroll, False)
        return carry

    lax.fori_loop(0, nk // unroll - 1, body, 0)
    group(nk - unroll, True)
    acc = acc_ref[...]
    o_ref[0] = (acc[:dv] / acc[dv:dv + 1]).T.astype(BF16)


def flash_attn(qt, k, vt, dv, tq, tk):
    batch, n_heads, dk, seq_len = qt.shape
    dve = vt.shape[1] // n_heads
    nk = seq_len // tk
    unroll = min(FLASH_UNROLL, nk)
    assert unroll % 2 == 0 and nk % unroll == 0
    return pl.pallas_call(
        functools.partial(_flash_kernel, tk=tk, nk=nk, dv=dv, unroll=unroll),
        grid=(batch, n_heads, seq_len // tq),
        in_specs=[pl.BlockSpec((1, 1, dk, tq), lambda b, h, i: (b, h, 0, i)),
                  pl.BlockSpec((1, 1, seq_len, dk), lambda b, h, i: (b, h, 0, 0)),
                  pl.BlockSpec((1, dve, seq_len), lambda b, h, i: (b, h, 0))],
        out_specs=pl.BlockSpec((1, tq, dv), lambda b, h, i: (b, i, h)),
        out_shape=jax.ShapeDtypeStruct((batch, seq_len, n_heads * dv), BF16),
        scratch_shapes=[pltpu.VMEM((1, tq), F32), pltpu.VMEM((dve, tq), F32),
                        pltpu.VMEM((tk, tq), F32), pltpu.VMEM((tk, tq), F32),
                        pltpu.VMEM((1, tq), F32), pltpu.VMEM((1, tq), F32)],
        compiler_params=_params(3),
        name="flash_attn",
    )(qt, k, vt)


def _rows_to_block(rows):
    n = rows[0].shape[1]
    rid = lax.broadcasted_iota(jnp.int32, (SUBLANES, n), 0)
    blk = jnp.broadcast_to(rows[-1], (SUBLANES, n))
    for k in range(SUBLANES - 2, -1, -1):
        blk = jnp.where(rid == k, rows[k], blk)
    return blk


def _ssm_kernel(u_ref, toep_ref, w1_ref, w2_ref, dec_ref, y_ref, s_ref, xf_ref, xb_ref, *, n_seq, n_chunk, n_state):
    u = u_ref[0]
    s_ref[...] = _dot(u, w1_ref[0])
    dec = dec_ref[0]
    a_re, a_im = dec[0:1], dec[1:2]
    fwd_lane = lax.broadcasted_iota(jnp.int32, (1, LANES), 1) < n_state
    n_blk = n_chunk // SUBLANES

    def step(i, carry):
        new = []
        for q in range(n_seq):
            x_re, x_im = carry[2 * q], carry[2 * q + 1]
            rf = pl.multiple_of(q * n_chunk + i * SUBLANES, SUBLANES)
            rb = pl.multiple_of(q * n_chunk + (n_blk - 1 - i) * SUBLANES, SUBLANES)
            sf = s_ref[pl.ds(rf, SUBLANES), :]
            sb = s_ref[pl.ds(rb, SUBLANES), :]
            rows_re, rows_im = [], []
            for k in range(SUBLANES):
                kb = SUBLANES - 1 - k
                rows_re.append(x_re)
                rows_im.append(x_im)
                s_re = jnp.where(fwd_lane, sf[k:k + 1, :LANES], sb[kb:kb + 1, :LANES])
                s_im = jnp.where(fwd_lane, sf[k:k + 1, LANES:], sb[kb:kb + 1, LANES:])
                x_re, x_im = a_re * x_re - a_im * x_im + s_re, a_re * x_im + a_im * x_re + s_im
            xf_ref[pl.ds(rf, SUBLANES), :LANES] = _rows_to_block(rows_re)
            xf_ref[pl.ds(rf, SUBLANES), LANES:] = _rows_to_block(rows_im)
            xb_ref[pl.ds(rb, SUBLANES), :LANES] = _rows_to_block(rows_re[::-1])
            xb_ref[pl.ds(rb, SUBLANES), LANES:] = _rows_to_block(rows_im[::-1])
            new += [x_re, x_im]
        return tuple(new)

    zero = jnp.zeros((1, LANES), F32)
    lax.fori_loop(0, n_blk, step, (zero,) * (2 * n_seq))
    y_ref[0] = (_dot(u, toep_ref[0]) + _dot(xf_ref[...].astype(BF16), w2_ref[0, 0])
                + _dot(xb_ref[...].astype(BF16), w2_ref[0, 1])).astype(y_ref.dtype)


def ssm(u_rows, toep, w1, w2, dec, n_seq, n_chunk):
    groups, rows, tp = u_rows.shape
    n_state = w1.shape[2] // 4
    assert 2 * n_state == LANES and n_chunk % SUBLANES == 0
    g3 = lambda g: (g, 0, 0)
    return pl.pallas_call(
        functools.partial(_ssm_kernel, n_seq=n_seq, n_chunk=n_chunk, n_state=n_state),
        grid=(groups,),
        in_specs=[pl.BlockSpec((1, rows, tp), g3), pl.BlockSpec((1, tp, tp), g3),
                  pl.BlockSpec((1, tp, 4 * n_state), g3), pl.BlockSpec((1, 2, 4 * n_state, tp), lambda g: (g, 0, 0, 0)),
                  pl.BlockSpec((1, 2, LANES), g3)],
        out_specs=pl.BlockSpec((1, rows, tp), g3),
        out_shape=jax.ShapeDtypeStruct((groups, rows, tp), BF16),
        scratch_shapes=[pltpu.VMEM((rows, 4 * n_state), F32)] * 3,
        compiler_params=_params(1),
        name="ssm",
    )(u_rows, toep, w1, w2, dec)


def ssm_operators(a_re, a_im, log_step, b_re, b_im, c_re, c_im, d_skip, chunk):
    _, groups, n_state = a_re.shape
    p = b_re.shape[-1]
    step = jnp.exp(log_step)[..., None]
    mag = jnp.exp(step * a_re)
    ab_re, ab_im = mag * jnp.cos(step * a_im), mag * jnp.sin(step * a_im)
    den = a_re * a_re + a_im * a_im
    nr, ni = ab_re - 1.0, ab_im
    f_re = (nr * a_re + ni * a_im) / den
    f_im = (ni * a_re - nr * a_im) / den
    bb_re = f_re[..., None] * b_re - f_im[..., None] * b_im
    bb_im = f_re[..., None] * b_im + f_im[..., None] * b_re
    j = jnp.arange(chunk + 1, dtype=F32)[:, None, None, None]
    pw_mag = jnp.exp(j * (step * a_re)[None])
    pw_re = pw_mag * jnp.cos(j * (step * a_im)[None])
    pw_im = pw_mag * jnp.sin(j * (step * a_im)[None])
    ab_b_re = pw_re[..., None] * bb_re[None] - pw_im[..., None] * bb_im[None]
    ab_b_im = pw_re[..., None] * bb_im[None] + pw_im[..., None] * bb_re[None]
    taps = (jnp.einsum('dgqn,jdgnp->jdgqp', c_re, ab_b_re[:chunk])
            - jnp.einsum('dgqn,jdgnp->jdgqp', c_im, ab_b_im[:chunk]))
    diag = (taps[0, 0] + taps[0, 1] + d_skip.reshape(groups, p)[:, :, None] * jnp.eye(p, dtype=F32))
    by_lag = jnp.concatenate([taps[chunk - 1:0:-1, 1], diag[None], taps[1:chunk, 0]], axis=0)
    by_lag = by_lag.transpose(1, 3, 0, 2).astype(BF16)
    toep = jnp.stack([by_lag[:, :, chunk - 1 - s:2 * chunk - 1 - s] for s in range(chunk)], axis=1)
    toep = toep.reshape(groups, chunk * p, chunk * p)
    e_f = jnp.arange(chunk - 1, -1, -1)
    e_b = jnp.arange(chunk)
    w1 = jnp.stack([ab_b_re[e_f, 0], ab_b_re[e_b, 1], ab_b_im[e_f, 0], ab_b_im[e_b, 1]], axis=0)
    w1 = w1.transpose(2, 1, 4, 0, 3).reshape(groups, chunk * p, 4 * n_state)
    o_f = jnp.arange(1, chunk + 1)
    o_b = jnp.arange(chunk, 0, -1)
    cp_re = c_re[None] * pw_re[:, :, :, None, :] - c_im[None] * pw_im[:, :, :, None, :]
    cp_im = c_re[None] * pw_im[:, :, :, None, :] + c_im[None] * pw_re[:, :, :, None, :]
    zf = jnp.zeros_like(cp_re[o_f, 0])
    w2 = jnp.stack([jnp.stack([cp_re[o_f, 0], zf, -cp_im[o_f, 0], zf], axis=0),
                    jnp.stack([zf, cp_re[o_b, 1], zf, -cp_im[o_b, 1]], axis=0)], axis=0)
    w2 = w2.transpose(3, 0, 1, 5, 2, 4).reshape(groups, 2, 4 * n_state, chunk * p)
    dec_re = jnp.concatenate([pw_re[chunk, 0], pw_re[chunk, 1]], axis=-1)
    dec_im = jnp.concatenate([pw_im[chunk, 0], pw_im[chunk, 1]], axis=-1)
    dec = jnp.stack([dec_re, dec_im], axis=1)
    return toep, w1.astype(BF16), w2.astype(BF16), dec


def _mix_kernel(xn_ref, o_ref, y_ref, wga_ref, wgs_ref, wo_ref, wz1_ref, wz2_ref, mix_ref, gy_ref):
    @pl.when(pl.program_id(1) == 0)
    def _():
        gy_ref[...] = jax.nn.gelu(y_ref[...].astype(F32)).astype(BF16)

    xn = xn_ref[...]
    gy = gy_ref[...]
    o_a = _dot(o_ref[...], wo_ref[...])
    o_s = _dot(gy, wz1_ref[...]) * jax.nn.sigmoid(_dot(gy, wz2_ref[...]))
    g_a = jax.nn.sigmoid(_dot(xn, wga_ref[...]))
    g_s = jax.nn.sigmoid(_dot(xn, wgs_ref[...]))
    mix_ref[...] = (g_a * o_a + g_s * o_s).astype(BF16)


def mix_merge(xn, o, y, wga, wgs, wo, wz1, wz2, tm, tn):
    t, d = xn.shape
    ho, sw = o.shape[1], y.shape[1]
    row = lambda i, j: (i, 0)
    col = lambda i, j: (0, j)
    return pl.pallas_call(
        _mix_kernel,
        grid=(t // tm, d // tn),
        in_specs=[pl.BlockSpec((tm, d), row), pl.BlockSpec((tm, ho), row), pl.BlockSpec((tm, sw), row),
                  pl.BlockSpec((d, tn), col), pl.BlockSpec((d, tn), col), pl.BlockSpec((ho, tn), col),
                  pl.BlockSpec((sw, tn), col), pl.BlockSpec((sw, tn), col)],
        out_specs=pl.BlockSpec((tm, tn), lambda i, j: (i, j)),
        out_shape=jax.ShapeDtypeStruct((t, d), BF16),
        scratch_shapes=[pltpu.VMEM((tm, sw), BF16)],
        compiler_params=_params(2, 1),
        name="mix_merge",
    )(xn, o, y, wga, wgs, wo, wz1, wz2)


def _resid_norm_kernel(x_ref, mix_ref, w_ref, g_ref, h_ref, n_ref):
    h = x_ref[...] + _dot(mix_ref[...], w_ref[...])
    h_ref[...] = h
    n_ref[...] = _rms(h, g_ref[...]).astype(BF16)


def resid_norm(x, mix, w, g, tm):
    t, d = x.shape
    row = lambda i: (i, 0)
    return pl.pallas_call(
        _resid_norm_kernel,
        grid=(t // tm,),
        in_specs=[pl.BlockSpec((tm, d), row), pl.BlockSpec((tm, d), row), _resident((d, d)), _resident((1, d))],
        out_specs=[pl.BlockSpec((tm, d), row), pl.BlockSpec((tm, d), row)],
        out_shape=[jax.ShapeDtypeStruct((t, d), F32), jax.ShapeDtypeStruct((t, d), BF16)],
        compiler_params=_params(1),
        name="resid_norm",
    )(x, mix, w, g.reshape(1, d))


def _mlp_kernel(n_ref, h_ref, w1_ref, w2_ref, o_ref, acc_ref):
    j = pl.program_id(1)

    @pl.when(j == 0)
    def _():
        acc_ref[...] = h_ref[...]

    a = jnp.maximum(_dot(n_ref[...], w1_ref[...]), 0.0)
    acc_ref[...] += _dot((a * a).astype(BF16), w2_ref[...])

    @pl.when(j == pl.num_programs(1) - 1)
    def _():
        o_ref[...] = acc_ref[...]


def mlp(n, h, w1, w2, tm, tf):
    t, d = h.shape
    ff = w1.shape[1]
    row = lambda i, j: (i, 0)
    return pl.pallas_call(
        _mlp_kernel,
        grid=(t // tm, ff // tf),
        in_specs=[pl.BlockSpec((tm, d), row), pl.BlockSpec((tm, d), row),
                  pl.BlockSpec((d, tf), lambda i, j: (0, j)), pl.BlockSpec((tf, d), lambda i, j: (j, 0))],
        out_specs=pl.BlockSpec((tm, d), row),
        out_shape=jax.ShapeDtypeStruct((t, d), F32),
        scratch_shapes=[pltpu.VMEM((tm, d), F32)],
        compiler_params=_params(2, 1),
        name="mlp",
    )(n, h, w1, w2)


def _ple_kernel(h_ref, p_ref, wg_ref, wp_ref, gp_ref, gf_ref, o_ref, *, final):
    h = h_ref[...]
    gate = jax.nn.sigmoid(_dot(_rms(h, gp_ref[...]).astype(BF16), wg_ref[...]))
    h = h + gate * _dot(p_ref[...].astype(BF16), wp_ref[...])
    o_ref[...] = _rms(h, gf_ref[...]) if final else h


def ple(h, p, wg, wp, gp, gf, tm, final):
    t, d = h.shape
    pd = p.shape[1]
    row = lambda i: (i, 0)
    return pl.pallas_call(
        functools.partial(_ple_kernel, final=final),
        grid=(t // tm,),
        in_specs=[pl.BlockSpec((tm, d), row), pl.BlockSpec((tm, pd), row), _resident((d, d)), _resident((pd, d)),
                  _resident((1, d)), _resident((1, d))],
        out_specs=pl.BlockSpec((tm, d), row),
        out_shape=jax.ShapeDtypeStruct((t, d), F32),
        compiler_params=_params(1),
        name="ple",
    )(h, p, wg, wp, gp.reshape(1, d), gf.reshape(1, d))


def _rope_tables(length, rope_dim):
    pos = jnp.arange(length, dtype=F32)
    inv = ROPE_THETA ** (-jnp.arange(0, rope_dim, 2, dtype=F32) / rope_dim)
    ang = pos[:, None] * inv[None, :]
    return jnp.cos(ang), jnp.sin(ang)


def _rope_lanes(cos, sin):
    length, half = cos.shape
    z = jnp.zeros((length, LANES // 2 - half), F32)
    return (jnp.concatenate([cos, z, cos, z], axis=1), jnp.concatenate([-sin, z, sin, z], axis=1))


def _spread_rope(w, half):
    z = jnp.zeros(w.shape[:-1] + (LANES // 2 - half,), w.dtype)
    return jnp.concatenate([w[..., :half], z, w[..., half:], z], axis=-1)


def _tile(n, pref):
    t = min(n, pref)
    while n % t:
        t //= 2
    return t


def _layer_weights(w_in, norm_q, w_uq, norm_kv, w_uk, w_uv, w_o_attn, ssm, w_glu, d_model):
    cq, ckv = norm_q.shape[0], norm_kv.shape[0]
    n_heads, dn = w_uk.shape[1], w_uk.shape[2]
    dv = w_uv.shape[2]
    sw = ssm[7].shape[0]
    rope = w_in.shape[1] - (cq + ckv + sw + 2 * d_model)
    half = rope // 2
    o_q, o_kv, o_kr, o_u, o_ga, o_gs = 0, cq, cq + ckv, cq + ckv + rope, cq + ckv + rope + sw, cq + ckv + rope + sw + d_model
    scale = (dn + rope) ** -0.5 * math.log2(math.e)
    wuq = w_uq.reshape(cq, n_heads, dn + rope) * scale
    wuq_t = jnp.concatenate([wuq[..., :dn], _spread_rope(wuq[..., dn:], half)], axis=-1).transpose(1, 2, 0)
    return dict(
        rope=rope, n_heads=n_heads, dn=dn, dv=dv,
        wq=w_in[:, o_q:o_kv].astype(BF16), wkv=w_in[:, o_kv:o_kr].astype(BF16),
        wkr=_spread_rope(w_in[:, o_kr:o_u], half).astype(BF16),
        wu=w_in[:, o_u:o_ga].astype(BF16), wga=w_in[:, o_ga:o_gs].astype(BF16), wgs=w_in[:, o_gs:].astype(BF16),
        wuq_t=wuq_t.astype(BF16),
        wuk=w_uk.reshape(ckv, n_heads * dn).astype(BF16),
        wuv_t=w_uv.reshape(ckv, n_heads * dv).T.astype(BF16),
        wo=w_o_attn.astype(BF16),
        wz1=w_glu[:, :d_model].astype(BF16), wz2=w_glu[:, d_model:].astype(BF16),
        ssm_ops=ssm_operators(*ssm, SSM_CHUNK),
    )


def _encoder_layer(h, p, lw, norm_mix, norm_q, norm_kv, w_out, norm_mlp, w_mlp_in, w_mlp_out,
                   norm_ple, w_ple_gate, w_ple, norm_out, final, batch, seq_len):
    t, d = h.shape
    tm = _tile(seq_len, 512)
    cos, sin = _rope_tables(seq_len, lw['rope'])
    cos_l, sin_l = _rope_lanes(cos, sin)
    xn, qn, ckv, kr, u = in_proj(h, norm_mix, lw['wq'], lw['wkv'], lw['wkr'], lw['wu'], norm_q, norm_kv,
                                 cos_l, sin_l, seq_len, tm)
    qt, k, vt = attn_prep(qn, ckv, kr, lw['wuq_t'], lw['wuk'], lw['wuv_t'], cos_l.T, sin_l.T, batch, seq_len, tm)
    o = flash_attn(qt, k, vt, lw['dv'], _tile(seq_len, FLASH_TQ), _tile(seq_len, FLASH_TK)).reshape(t, -1)
    toep, w1, w2, dec = lw['ssm_ops']
    groups = toep.shape[0]
    gp = lw['wu'].shape[1] // groups
    n_chunk = seq_len // SSM_CHUNK
    u_rows = (u.reshape(batch * n_chunk, SSM_CHUNK, groups, gp).transpose(2, 0, 1, 3)
              .reshape(groups, batch * n_chunk, SSM_CHUNK * gp))
    y_rows = ssm(u_rows, toep, w1, w2, dec, batch, n_chunk)
    y = (y_rows.reshape(groups, batch * n_chunk, SSM_CHUNK, gp).transpose(1, 2, 0, 3).reshape(t, groups * gp))
    mix = mix_merge(xn, o, y, lw['wga'], lw['wgs'], lw['wo'], lw['wz1'], lw['wz2'], tm, _tile(d, 512))
    h1, n1 = resid_norm(h, mix, w_out.astype(BF16), norm_mlp, _tile(seq_len, 256))
    h2 = mlp(n1, h1, w_mlp_in.astype(BF16), w_mlp_out.astype(BF16), tm, _tile(w_mlp_in.shape[1], 1024))
    return ple(h2, p, w_ple_gate.astype(BF16), w_ple.astype(BF16), norm_ple, norm_out, _tile(seq_len, 256), final)


def kernel(x_prompt, x_sample, p_prompt, p_sample, norm_mix, w_in, norm_q, w_uq, norm_kv, w_uk, w_uv, w_o_attn, ssm_a_re, ssm_a_im, ssm_log_step, ssm_b_re, ssm_b_im, ssm_c_re, ssm_c_im, ssm_d, w_glu, w_out, norm_mlp, w_mlp_in, w_mlp_out, norm_ple, w_ple_gate, w_ple, norm_final):
    depth = w_in.shape[0]
    d_model = x_prompt.shape[-1]
    layers = []
    for i in range(depth):
        ssm_w = (ssm_a_re[i], ssm_a_im[i], ssm_log_step[i], ssm_b_re[i], ssm_b_im[i], ssm_c_re[i], ssm_c_im[i], ssm_d[i])
        layers.append(_layer_weights(w_in[i], norm_q[i], w_uq[i], norm_kv[i], w_uk[i], w_uv[i], w_o_attn[i],
                                     ssm_w, w_glu[i], d_model))
    outs = []
    for x, p in ((x_prompt, p_prompt), (x_sample, p_sample)):
        batch, seq_len, _ = x.shape
        h = x.reshape(batch * seq_len, d_model)
        for i in range(depth):
            final = i == depth - 1
            h = _encoder_layer(h, p[i].reshape(batch * seq_len, -1), layers[i], norm_mix[i], norm_q[i], norm_kv[i],
                               w_out[i], norm_mlp[i], w_mlp_in[i], w_mlp_out[i], norm_ple[i], w_ple_gate[i], w_ple[i],
                               norm_final if final else norm_ple[i], final, batch, seq_len)
        outs.append(h.reshape(batch, seq_len, d_model))
    return tuple(outs)
```

```python
import functools
import math

import jax
import jax.numpy as jnp
from jax import lax
from jax.experimental import pallas as pl
from jax.experimental.pallas import tpu as pltpu

NORM_EPS = 1e-6
ROPE_THETA = 10000.0
LANES = 128
SUBLANES = 8
BF16_ROWS = 16
FLASH_TQ = 1024
FLASH_TK = 512
FLASH_UNROLL = 8
MXU_DIM = 256
VMEM_LIMIT = 56 * 1024 * 1024
SSM_CHUNK = 32
BF16 = jnp.bfloat16
F32 = jnp.float32


def _params(n_axes, n_parallel=None):
    n_parallel = n_axes if n_parallel is None else n_parallel
    sem = ("parallel",) * n_parallel + ("arbitrary",) * (n_axes - n_parallel)
    return pltpu.CompilerParams(dimension_semantics=sem, vmem_limit_bytes=VMEM_LIMIT)


def _resident(shape):
    nd = len(shape)
    return pl.BlockSpec(shape, lambda *_: (0,) * nd, pipeline_mode=pl.Buffered(1))


def _rms(x, g):
    r = lax.rsqrt(jnp.mean(x * x, axis=-1, keepdims=True) + NORM_EPS)
    return (x * r) * g


def _dot(a, b):
    return jnp.dot(a, b, preferred_element_type=F32)


def _dot_nt(a, b):
    return lax.dot_general(a, b, (((1,), (1,)), ((), ())), preferred_element_type=F32)


def _in_proj_kernel(x_ref, g_ref, wq_ref, wkv_ref, wkr_ref, wu_ref, gq_ref, gkv_ref, cos_ref, sin_ref,
                    xn_ref, qn_ref, ckv_ref, kr_ref, u_ref):
    xn = _rms(x_ref[...], g_ref[...]).astype(BF16)
    xn_ref[...] = xn
    qn_ref[...] = _rms(_dot(xn, wq_ref[...]), gq_ref[...]).astype(BF16)
    ckv_ref[...] = _rms(_dot(xn, wkv_ref[...]), gkv_ref[...]).astype(BF16)
    kr = _dot(xn, wkr_ref[...])
    kr_ref[...] = (kr * cos_ref[...] + pltpu.roll(kr, LANES // 2, 1) * sin_ref[...]).astype(BF16)
    u_ref[...] = _dot(xn, wu_ref[...]).astype(BF16)


def in_proj(x, g, wq, wkv, wkr, wu, gq, gkv, cos_l, sin_l, seq_len, tm):
    t, d = x.shape
    cq, ckv, sw = wq.shape[1], wkv.shape[1], wu.shape[1]
    npos = seq_len // tm
    row = lambda i: (i, 0)
    pos = lambda i: (i % npos, 0)
    widths = (d, cq, ckv, LANES, sw)
    return pl.pallas_call(
        _in_proj_kernel,
        grid=(t // tm,),
        in_specs=[pl.BlockSpec((tm, d), row), _resident((1, d)),
                  _resident((d, cq)), _resident((d, ckv)), _resident((d, LANES)), _resident((d, sw)),
                  _resident((1, cq)), _resident((1, ckv)),
                  pl.BlockSpec((tm, LANES), pos), pl.BlockSpec((tm, LANES), pos)],
        out_specs=[pl.BlockSpec((tm, w), row) for w in widths],
        out_shape=[jax.ShapeDtypeStruct((t, w), BF16) for w in widths],
        compiler_params=_params(1),
        name="in_proj",
    )(x, g.reshape(1, d), wq, wkv, wkr, wu, gq.reshape(1, cq), gkv.reshape(1, ckv), cos_l, sin_l)


def _attn_prep_kernel(qn_ref, ckv_ref, kr_ref, wuq_ref, wuk_ref, wuv_ref, cos_ref, sin_ref,
                      qt_ref, k_ref, vt_ref, *, n_heads, dn, dv):
    qn = qn_ref[...]
    ckv = ckv_ref[...]
    cos_t = cos_ref[...]
    sin_t = sin_ref[...]
    half = LANES // 2
    for h in range(n_heads):
        q = _dot_nt(wuq_ref[h], qn)
        qr = q[dn:]
        qr_sw = jnp.concatenate([qr[half:], qr[:half]], axis=0)
        qt_ref[0, h, :dn, :] = q[:dn].astype(BF16)
        qt_ref[0, h, dn:, :] = (qr * cos_t + qr_sw * sin_t).astype(BF16)
    k_all = _dot(ckv, wuk_ref[...])
    kr = kr_ref[...]
    for h in range(n_heads):
        k_ref[0, h, :, :dn] = k_all[:, h * dn:(h + 1) * dn].astype(BF16)
        k_ref[0, h, :, dn:] = kr
    vt = _dot_nt(wuv_ref[...], ckv).astype(BF16)
    tm = vt.shape[1]
    ones_rows = (lax.broadcasted_iota(jnp.int32, (BF16_ROWS, tm), 0) == 0).astype(BF16)
    dve = dv + BF16_ROWS
    for h in range(n_heads):
        vt_ref[0, h * dve:h * dve + dv, :] = vt[h * dv:(h + 1) * dv]
        vt_ref[0, h * dve + dv:(h + 1) * dve, :] = ones_rows


def attn_prep(qn, ckv, kr, wuq_t, wuk, wuv_t, cos_t, sin_t, batch, seq_len, tm):
    n_heads, dk, cq = wuq_t.shape
    dn = dk - LANES
    dv = wuv_t.shape[0] // n_heads
    hdv = n_heads * (dv + BF16_ROWS)
    ckv_dim = ckv.shape[1]
    npos = seq_len // tm
    row = lambda b, i: (b * npos + i, 0)
    col = lambda b, i: (0, i)
    return pl.pallas_call(
        functools.partial(_attn_prep_kernel, n_heads=n_heads, dn=dn, dv=dv),
        grid=(batch, npos),
        in_specs=[pl.BlockSpec((tm, cq), row), pl.BlockSpec((tm, ckv_dim), row), pl.BlockSpec((tm, LANES), row),
                  _resident((n_heads, dk, cq)), _resident((ckv_dim, n_heads * dn)), _resident(wuv_t.shape),
                  pl.BlockSpec((LANES, tm), col), pl.BlockSpec((LANES, tm), col)],
        out_specs=[pl.BlockSpec((1, n_heads, dk, tm), lambda b, i: (b, 0, 0, i)),
                   pl.BlockSpec((1, n_heads, tm, dk), lambda b, i: (b, 0, i, 0)),
                   pl.BlockSpec((1, hdv, tm), lambda b, i: (b, 0, i))],
        out_shape=[jax.ShapeDtypeStruct((batch, n_heads, dk, seq_len), BF16),
                   jax.ShapeDtypeStruct((batch, n_heads, seq_len, dk), BF16),
                   jax.ShapeDtypeStruct((batch, hdv, seq_len), BF16)],
        compiler_params=_params(2),
        name="attn_prep",
    )(qn, ckv, kr, wuq_t, wuk, wuv_t, cos_t, sin_t)


def _flash_kernel(qt_ref, k_ref, vt_ref, o_ref, m_ref, acc_ref, s0_ref, s1_ref, c0_ref, c1_ref,
                  *, tk, nk, dv, unroll):
    qt = qt_ref[0, 0]
    tq = qt.shape[1]
    m_ref[...] = jnp.full(m_ref.shape, -jnp.inf, F32)
    acc_ref[...] = jnp.zeros(acc_ref.shape, F32)

    def scores(j, s_ref, c_ref):
        off = pl.multiple_of(j * tk, tk)
        k = k_ref[0, 0, pl.ds(off, tk), :]
        for c in range(0, tq, MXU_DIM):
            s = _dot(k, qt[:, c:c + MXU_DIM])
            s_ref[:, c:c + MXU_DIM] = s
            c_ref[:, c:c + MXU_DIM] = jnp.max(s, axis=0, keepdims=True)

    def consume(j, s_ref, c_ref):
        off = pl.multiple_of(j * tk, tk)
        vt = vt_ref[0, :, pl.ds(off, tk)]
        for c in range(0, tq, MXU_DIM):
            cols = slice(c, c + MXU_DIM)
            m_prev = m_ref[:, cols]
            m_new = jnp.maximum(m_prev, c_ref[:, cols])
            alpha = jnp.exp2(m_prev - m_new)
            p = jnp.exp2(s_ref[:, cols] - m_new).astype(BF16)
            acc_ref[:, cols] = alpha * acc_ref[:, cols] + _dot(vt, p)
            m_ref[:, cols] = m_new

    bufs = ((s0_ref, c0_ref), (s1_ref, c1_ref))
    scores(0, *bufs[0])

    def group(base, last):
        for u in range(unroll):
            if not (last and u == unroll - 1):
                scores(base + u + 1, *bufs[(u + 1) % 2])
            consume(base + u, *bufs[u % 2])

    def body(i, carry):
        group(i * unroll, False)
        return carry

    lax.fori_loop(0, nk // unroll - 1, body, 0)
    group(nk - unroll, True)
    acc = acc_ref[...]
    o_ref[0] = (acc[:dv] / acc[dv:dv + 1]).T.astype(BF16)


def flash_attn(qt, k, vt, dv, tq, tk):
    batch, n_heads, dk, seq_len = qt.shape
    dve = vt.shape[1] // n_heads
    nk = seq_len // tk
    unroll = min(FLASH_UNROLL, nk)
    assert unroll % 2 == 0 and nk % unroll == 0
    return pl.pallas_call(
        functools.partial(_flash_kernel, tk=tk, nk=nk, dv=dv, unroll=unroll),
        grid=(batch, n_heads, seq_len // tq),
        in_specs=[pl.BlockSpec((1, 1, dk, tq), lambda b, h, i: (b, h, 0, i)),
                  pl.BlockSpec((1, 1, seq_len, dk), lambda b, h, i: (b, h, 0, 0)),
                  pl.BlockSpec((1, dve, seq_len), lambda b, h, i: (b, h, 0))],
        out_specs=pl.BlockSpec((1, tq, dv), lambda b, h, i: (b, i, h)),
        out_shape=jax.ShapeDtypeStruct((batch, seq_len, n_heads * dv), BF16),
        scratch_shapes=[pltpu.VMEM((1, tq), F32), pltpu.VMEM((dve, tq), F32),
                        pltpu.VMEM((tk, tq), F32), pltpu.VMEM((tk, tq), F32),
                        pltpu.VMEM((1, tq), F32), pltpu.VMEM((1, tq), F32)],
        compiler_params=_params(3),
        name="flash_attn",
    )(qt, k, vt)


def _rows_to_block(rows):
    n = rows[0].shape[1]
    rid = lax.broadcasted_iota(jnp.int32, (SUBLANES, n), 0)
    blk = jnp.broadcast_to(rows[-1], (SUBLANES, n))
    for k in range(SUBLANES - 2, -1, -1):
        blk = jnp.where(rid == k, rows[k], blk)
    return blk


def _ssm_kernel(u_ref, toep_ref, w1_ref, w2_ref, dec_ref, y_ref, s_ref, xf_ref, xb_ref, *, n_seq, n_chunk, n_state):
    u = u_ref[0]
    s_ref[...] = _dot(u, w1_ref[0])
    dec = dec_ref[0]
    a_re, a_im = dec[0:1], dec[1:2]
    fwd_lane = lax.broadcasted_iota(jnp.int32, (1, LANES), 1) < n_state
    n_blk = n_chunk // SUBLANES

    def step(i, carry):
        new = []
        for q in range(n_seq):
            x_re, x_im = carry[2 * q], carry[2 * q + 1]
            rf = pl.multiple_of(q * n_chunk + i * SUBLANES, SUBLANES)
            rb = pl.multiple_of(q * n_chunk + (n_blk - 1 - i) * SUBLANES, SUBLANES)
            sf = s_ref[pl.ds(rf, SUBLANES), :]
            sb = s_ref[pl.ds(rb, SUBLANES), :]
            rows_re, rows_im = [], []
            for k in range(SUBLANES):
                kb = SUBLANES - 1 - k
                rows_re.append(x_re)
                rows_im.append(x_im)
                s_re = jnp.where(fwd_lane, sf[k:k + 1, :LANES], sb[kb:kb + 1, :LANES])
                s_im = jnp.where(fwd_lane, sf[k:k + 1, LANES:], sb[kb:kb + 1, LANES:])
                x_re, x_im = a_re * x_re - a_im * x_im + s_re, a_re * x_im + a_im * x_re + s_im
            xf_ref[pl.ds(rf, SUBLANES), :LANES] = _rows_to_block(rows_re)
            xf_ref[pl.ds(rf, SUBLANES), LANES:] = _rows_to_block(rows_im)
            xb_ref[pl.ds(rb, SUBLANES), :LANES] = _rows_to_block(rows_re[::-1])
            xb_ref[pl.ds(rb, SUBLANES), LANES:] = _rows_to_block(rows_im[::-1])
            new += [x_re, x_im]
        return tuple(new)

    zero = jnp.zeros((1, LANES), F32)
    lax.fori_loop(0, n_blk, step, (zero,) * (2 * n_seq))
    y_ref[0] = (_dot(u, toep_ref[0]) + _dot_nt(xf_ref[...].astype(BF16), w2_ref[0, 0])
                + _dot_nt(xb_ref[...].astype(BF16), w2_ref[0, 1])).astype(y_ref.dtype)


def ssm(u_rows, toep, w1, w2, dec, n_seq, n_chunk):
    groups, rows, tp = u_rows.shape
    n_state = w1.shape[2] // 4
    assert 2 * n_state == LANES and n_chunk % SUBLANES == 0
    g3 = lambda g: (g, 0, 0)
    return pl.pallas_call(
        functools.partial(_ssm_kernel, n_seq=n_seq, n_chunk=n_chunk, n_state=n_state),
        grid=(groups,),
        in_specs=[pl.BlockSpec((1, rows, tp), g3), pl.BlockSpec((1, tp, tp), g3),
                  pl.BlockSpec((1, tp, 4 * n_state), g3), pl.BlockSpec((1, 2, tp, 4 * n_state), lambda g: (g, 0, 0, 0)),
                  pl.BlockSpec((1, 2, LANES), g3)],
        out_specs=pl.BlockSpec((1, rows, tp), g3),
        out_shape=jax.ShapeDtypeStruct((groups, rows, tp), BF16),
        scratch_shapes=[pltpu.VMEM((rows, 4 * n_state), F32)] * 3,
        compiler_params=_params(1),
        name="ssm",
    )(u_rows, toep, w1, w2, dec)


def ssm_operators(a_re, a_im, log_step, b_re, b_im, c_re, c_im, d_skip, chunk):
    _, groups, n_state = a_re.shape
    p = b_re.shape[-1]
    step = jnp.exp(log_step)[..., None]
    mag = jnp.exp(step * a_re)
    ab_re, ab_im = mag * jnp.cos(step * a_im), mag * jnp.sin(step * a_im)
    den = a_re * a_re + a_im * a_im
    nr, ni = ab_re - 1.0, ab_im
    f_re = (nr * a_re + ni * a_im) / den
    f_im = (ni * a_re - nr * a_im) / den
    bb_re = f_re[..., None] * b_re - f_im[..., None] * b_im
    bb_im = f_re[..., None] * b_im + f_im[..., None] * b_re
    j = jnp.arange(chunk + 1, dtype=F32)[:, None, None, None]
    pw_mag = jnp.exp(j * (step * a_re)[None])
    pw_re = pw_mag * jnp.cos(j * (step * a_im)[None])
    pw_im = pw_mag * jnp.sin(j * (step * a_im)[None])
    def fb(x_f, x_b):
        return jnp.concatenate([x_f, x_b], axis=-1).transpose(1, 0, 2)

    def re_im(f):
        return jnp.stack([f(pw_re), f(pw_im)], axis=1)

    zero = jnp.zeros((chunk - 1, groups, n_state), F32)
    pw_lag = re_im(lambda w: fb(jnp.concatenate([zero, w[:chunk, 0]]), jnp.concatenate([w[chunk - 1::-1, 1], zero])))
    pw_lag = jnp.pad(pw_lag, ((0, 0), (0, 0), (0, 1), (0, 0)))
    pw_in = re_im(lambda w: fb(w[chunk - 1::-1, 0], w[:chunk, 1]))
    pw_out = re_im(lambda w: fb(w[1:chunk + 1, 0], w[chunk:0:-1, 1]))
    dec = jnp.stack([fb(pw_re[chunk:, 0], pw_re[chunk:, 1])[:, 0], fb(pw_im[chunk:, 0], pw_im[chunk:, 1])[:, 0]],
                    axis=1)
    bt = jnp.stack([jnp.concatenate([bb_re[0], bb_re[1]], axis=1), jnp.concatenate([bb_im[0], bb_im[1]], axis=1)],
                   axis=1).transpose(0, 1, 3, 2)
    cc = jnp.stack([jnp.concatenate([c_re[0], c_re[1]], axis=-1), jnp.concatenate([c_im[0], c_im[1]], axis=-1)],
                   axis=1)
    eye = jnp.eye(p, LANES, dtype=F32)
    skip = d_skip.reshape(groups, p)[:, :, None] * eye
    n_rows = -(-(2 * chunk - 1) * p // LANES) * LANES
    sel = jnp.zeros((n_rows, LANES), F32).at[(chunk - 1) * p:chunk * p].set(eye)
    toep, w1, w2 = ssm_tables(bt, cc, skip, sel.astype(BF16), pw_lag, pw_in, pw_out, chunk)
    return toep, w1, w2, dec


def _ssm_tables_kernel(bt_ref, c_ref, skip_ref, sel_ref, pwl_ref, pwi_ref, pwo_ref, toep_ref, w1_ref, w2_ref, rt_ref,
                       *, chunk, n_state):
    p = bt_ref.shape[2]
    bt_re, bt_im = bt_ref[0, 0], bt_ref[0, 1]
    c_re, c_im = c_ref[0, 0], c_ref[0, 1]
    rt_ref[...] = jnp.zeros(rt_ref.shape, BF16)
    for li in range(2 * chunk - 1):
        a_r, a_i = pwl_ref[0, 0, li:li + 1, :], pwl_ref[0, 1, li:li + 1, :]
        rt_ref[li * p:(li + 1) * p, :LANES] = (c_re * a_r - c_im * a_i).astype(BF16)
        rt_ref[li * p:(li + 1) * p, LANES:] = (c_re * a_i + c_im * a_r).astype(BF16)
    lhs = jnp.concatenate([bt_re, -bt_im], axis=1).astype(BF16)
    by_lag = _dot_nt(lhs, rt_ref[...]) + _dot_nt(skip_ref[0].astype(BF16), sel_ref[...])
    width = by_lag.shape[1]
    for s in range(chunk):
        off = (chunk - 1 - s) * p
        win = by_lag if off == 0 else pltpu.roll(by_lag, width - off, 1)
        toep_ref[0, s * p:(s + 1) * p, :] = win[:, :chunk * p].astype(BF16)
    fwd = lax.broadcasted_iota(jnp.int32, (p, LANES), 1) < n_state
    for s in range(chunk):
        rows = slice(s * p, (s + 1) * p)
        a_r, a_i = pwi_ref[0, 0, s:s + 1, :], pwi_ref[0, 1, s:s + 1, :]
        w1_ref[0, rows, :LANES] = (bt_re * a_r - bt_im * a_i).astype(BF16)
        w1_ref[0, rows, LANES:] = (bt_re * a_i + bt_im * a_r).astype(BF16)
        a_r, a_i = pwo_ref[0, 0, s:s + 1, :], pwo_ref[0, 1, s:s + 1, :]
        o_re = c_re * a_r - c_im * a_i
        o_im = -(c_re * a_i + c_im * a_r)
        w2_ref[0, 0, rows, :LANES] = jnp.where(fwd, o_re, 0.0).astype(BF16)
        w2_ref[0, 0, rows, LANES:] = jnp.where(fwd, o_im, 0.0).astype(BF16)
        w2_ref[0, 1, rows, :LANES] = jnp.where(fwd, 0.0, o_re).astype(BF16)
        w2_ref[0, 1, rows, LANES:] = jnp.where(fwd, 0.0, o_im).astype(BF16)


def ssm_tables(bt, cc, skip, sel, pw_lag, pw_in, pw_out, chunk):
    groups, _, p, lanes = bt.shape
    n_state = lanes // 2
    tp = chunk * p
    g4 = lambda g: (g, 0, 0, 0)
    blk4 = lambda a: pl.BlockSpec((1,) + a.shape[1:], g4)
    return pl.pallas_call(
        functools.partial(_ssm_tables_kernel, chunk=chunk, n_state=n_state),
        grid=(groups,),
        in_specs=[blk4(bt), blk4(cc), pl.BlockSpec((1, p, LANES), lambda g: (g, 0, 0)), _resident(sel.shape),
                  blk4(pw_lag), blk4(pw_in), blk4(pw_out)],
        out_specs=[pl.BlockSpec((1, tp, tp), lambda g: (g, 0, 0)), pl.BlockSpec((1, tp, 2 * lanes), lambda g: (g, 0, 0)),
                   pl.BlockSpec((1, 2, tp, 2 * lanes), g4)],
        out_shape=[jax.ShapeDtypeStruct((groups, tp, tp), BF16), jax.ShapeDtypeStruct((groups, tp, 2 * lanes), BF16),
                   jax.ShapeDtypeStruct((groups, 2, tp, 2 * lanes), BF16)],
        scratch_shapes=[pltpu.VMEM(sel.shape[:1] + (2 * lanes,), BF16)],
        compiler_params=_params(1),
        name="ssm_tables",
    )(bt, cc, skip, sel, pw_lag, pw_in, pw_out)


def _mix_kernel(xn_ref, o_ref, y_ref, wga_ref, wgs_ref, wo_ref, wz1_ref, wz2_ref, mix_ref, gy_ref):
    @pl.when(pl.program_id(1) == 0)
    def _():
        gy_ref[...] = jax.nn.gelu(y_ref[...].astype(F32)).astype(BF16)

    xn = xn_ref[...]
    gy = gy_ref[...]
    o_a = _dot(o_ref[...], wo_ref[...])
    o_s = _dot(gy, wz1_ref[...]) * jax.nn.sigmoid(_dot(gy, wz2_ref[...]))
    g_a = jax.nn.sigmoid(_dot(xn, wga_ref[...]))
    g_s = jax.nn.sigmoid(_dot(xn, wgs_ref[...]))
    mix_ref[...] = (g_a * o_a + g_s * o_s).astype(BF16)


def mix_merge(xn, o, y, wga, wgs, wo, wz1, wz2, tm, tn):
    t, d = xn.shape
    ho, sw = o.shape[1], y.shape[1]
    row = lambda i, j: (i, 0)
    col = lambda i, j: (0, j)
    return pl.pallas_call(
        _mix_kernel,
        grid=(t // tm, d // tn),
        in_specs=[pl.BlockSpec((tm, d), row), pl.BlockSpec((tm, ho), row), pl.BlockSpec((tm, sw), row),
                  pl.BlockSpec((d, tn), col), pl.BlockSpec((d, tn), col), pl.BlockSpec((ho, tn), col),
                  pl.BlockSpec((sw, tn), col), pl.BlockSpec((sw, tn), col)],
        out_specs=pl.BlockSpec((tm, tn), lambda i, j: (i, j)),
        out_shape=jax.ShapeDtypeStruct((t, d), BF16),
        scratch_shapes=[pltpu.VMEM((tm, sw), BF16)],
        compiler_params=_params(2, 1),
        name="mix_merge",
    )(xn, o, y, wga, wgs, wo, wz1, wz2)


def _resid_norm_kernel(x_ref, mix_ref, w_ref, g_ref, h_ref, n_ref):
    h = x_ref[...] + _dot(mix_ref[...], w_ref[...])
    h_ref[...] = h
    n_ref[...] = _rms(h, g_ref[...]).astype(BF16)


def resid_norm(x, mix, w, g, tm):
    t, d = x.shape
    row = lambda i: (i, 0)
    return pl.pallas_call(
        _resid_norm_kernel,
        grid=(t // tm,),
        in_specs=[pl.BlockSpec((tm, d), row), pl.BlockSpec((tm, d), row), _resident((d, d)), _resident((1, d))],
        out_specs=[pl.BlockSpec((tm, d), row), pl.BlockSpec((tm, d), row)],
        out_shape=[jax.ShapeDtypeStruct((t, d), F32), jax.ShapeDtypeStruct((t, d), BF16)],
        compiler_params=_params(1),
        name="resid_norm",
    )(x, mix, w, g.reshape(1, d))


def _mlp_kernel(n_ref, h_ref, w1_ref, w2_ref, o_ref, acc_ref):
    j = pl.program_id(1)

    @pl.when(j == 0)
    def _():
        acc_ref[...] = h_ref[...]

    a = jnp.maximum(_dot(n_ref[...], w1_ref[...]), 0.0)
    acc_ref[...] += _dot((a * a).astype(BF16), w2_ref[...])

    @pl.when(j == pl.num_programs(1) - 1)
    def _():
        o_ref[...] = acc_ref[...]


def mlp(n, h, w1, w2, tm, tf):
    t, d = h.shape
    ff = w1.shape[1]
    row = lambda i, j: (i, 0)
    return pl.pallas_call(
        _mlp_kernel,
        grid=(t // tm, ff // tf),
        in_specs=[pl.BlockSpec((tm, d), row), pl.BlockSpec((tm, d), row),
                  pl.BlockSpec((d, tf), lambda i, j: (0, j)), pl.BlockSpec((tf, d), lambda i, j: (j, 0))],
        out_specs=pl.BlockSpec((tm, d), row),
        out_shape=jax.ShapeDtypeStruct((t, d), F32),
        scratch_shapes=[pltpu.VMEM((tm, d), F32)],
        compiler_params=_params(2, 1),
        name="mlp",
    )(n, h, w1, w2)


def _ple_kernel(h_ref, p_ref, wg_ref, wp_ref, gp_ref, gf_ref, o_ref, *, final):
    h = h_ref[...]
    gate = jax.nn.sigmoid(_dot(_rms(h, gp_ref[...]).astype(BF16), wg_ref[...]))
    h = h + gate * _dot(p_ref[...].astype(BF16), wp_ref[...])
    o_ref[...] = _rms(h, gf_ref[...]) if final else h


def ple(h, p, wg, wp, gp, gf, tm, final):
    t, d = h.shape
    pd = p.shape[1]
    row = lambda i: (i, 0)
    return pl.pallas_call(
        functools.partial(_ple_kernel, final=final),
        grid=(t // tm,),
        in_specs=[pl.BlockSpec((tm, d), row), pl.BlockSpec((tm, pd), row), _resident((d, d)), _resident((pd, d)),
                  _resident((1, d)), _resident((1, d))],
        out_specs=pl.BlockSpec((tm, d), row),
        out_shape=jax.ShapeDtypeStruct((t, d), F32),
        compiler_params=_params(1),
        name="ple",
    )(h, p, wg, wp, gp.reshape(1, d), gf.reshape(1, d))


def _rope_tables(length, rope_dim):
    pos = jnp.arange(length, dtype=F32)
    inv = ROPE_THETA ** (-jnp.arange(0, rope_dim, 2, dtype=F32) / rope_dim)
    ang = pos[:, None] * inv[None, :]
    return jnp.cos(ang), jnp.sin(ang)


def _rope_lanes(cos, sin):
    length, half = cos.shape
    z = jnp.zeros((length, LANES // 2 - half), F32)
    return (jnp.concatenate([cos, z, cos, z], axis=1), jnp.concatenate([-sin, z, sin, z], axis=1))


def _spread_rope(w, half):
    z = jnp.zeros(w.shape[:-1] + (LANES // 2 - half,), w.dtype)
    return jnp.concatenate([w[..., :half], z, w[..., half:], z], axis=-1)


def _tile(n, pref):
    t = min(n, pref)
    while n % t:
        t //= 2
    return t


def _layer_weights(w_in, norm_q, w_uq, norm_kv, w_uk, w_uv, w_o_attn, ssm, w_glu, d_model):
    cq, ckv = norm_q.shape[0], norm_kv.shape[0]
    n_heads, dn = w_uk.shape[1], w_uk.shape[2]
    dv = w_uv.shape[2]
    sw = ssm[7].shape[0]
    rope = w_in.shape[1] - (cq + ckv + sw + 2 * d_model)
    half = rope // 2
    o_q, o_kv, o_kr, o_u, o_ga, o_gs = 0, cq, cq + ckv, cq + ckv + rope, cq + ckv + rope + sw, cq + ckv + rope + sw + d_model
    scale = (dn + rope) ** -0.5 * math.log2(math.e)
    wuq = w_uq.reshape(cq, n_heads, dn + rope) * scale
    wuq_t = jnp.concatenate([wuq[..., :dn], _spread_rope(wuq[..., dn:], half)], axis=-1).transpose(1, 2, 0)
    return dict(
        rope=rope, n_heads=n_heads, dn=dn, dv=dv,
        wq=w_in[:, o_q:o_kv].astype(BF16), wkv=w_in[:, o_kv:o_kr].astype(BF16),
        wkr=_spread_rope(w_in[:, o_kr:o_u], half).astype(BF16),
        wu=w_in[:, o_u:o_ga].astype(BF16), wga=w_in[:, o_ga:o_gs].astype(BF16), wgs=w_in[:, o_gs:].astype(BF16),
        wuq_t=wuq_t.astype(BF16),
        wuk=w_uk.reshape(ckv, n_heads * dn).astype(BF16),
        wuv_t=w_uv.reshape(ckv, n_heads * dv).T.astype(BF16),
        wo=w_o_attn.astype(BF16),
        wz1=w_glu[:, :d_model].astype(BF16), wz2=w_glu[:, d_model:].astype(BF16),
        ssm_ops=ssm_operators(*ssm, SSM_CHUNK),
    )


def _encoder_layer(h, p, lw, norm_mix, norm_q, norm_kv, w_out, norm_mlp, w_mlp_in, w_mlp_out,
                   norm_ple, w_ple_gate, w_ple, norm_out, final, batch, seq_len):
    t, d = h.shape
    tm = _tile(seq_len, 512)
    cos, sin = _rope_tables(seq_len, lw['rope'])
    cos_l, sin_l = _rope_lanes(cos, sin)
    xn, qn, ckv, kr, u = in_proj(h, norm_mix, lw['wq'], lw['wkv'], lw['wkr'], lw['wu'], norm_q, norm_kv,
                                 cos_l, sin_l, seq_len, tm)
    qt, k, vt = attn_prep(qn, ckv, kr, lw['wuq_t'], lw['wuk'], lw['wuv_t'], cos_l.T, sin_l.T, batch, seq_len, tm)
    o = flash_attn(qt, k, vt, lw['dv'], _tile(seq_len, FLASH_TQ), _tile(seq_len, FLASH_TK)).reshape(t, -1)
    toep, w1, w2, dec = lw['ssm_ops']
    groups = toep.shape[0]
    gp = lw['wu'].shape[1] // groups
    n_chunk = seq_len // SSM_CHUNK
    u_rows = (u.reshape(batch * n_chunk, SSM_CHUNK, groups, gp).transpose(2, 0, 1, 3)
              .reshape(groups, batch * n_chunk, SSM_CHUNK * gp))
    y_rows = ssm(u_rows, toep, w1, w2, dec, batch, n_chunk)
    y = (y_rows.reshape(groups, batch * n_chunk, SSM_CHUNK, gp).transpose(1, 2, 0, 3).reshape(t, groups * gp))
    mix = mix_merge(xn, o, y, lw['wga'], lw['wgs'], lw['wo'], lw['wz1'], lw['wz2'], tm, _tile(d, 512))
    h1, n1 = resid_norm(h, mix, w_out.astype(BF16), norm_mlp, tm)
    h2 = mlp(n1, h1, w_mlp_in.astype(BF16), w_mlp_out.astype(BF16), tm, _tile(w_mlp_in.shape[1], 1024))
    return ple(h2, p, w_ple_gate.astype(BF16), w_ple.astype(BF16), norm_ple, norm_out, tm, final)


def kernel(x_prompt, x_sample, p_prompt, p_sample, norm_mix, w_in, norm_q, w_uq, norm_kv, w_uk, w_uv, w_o_attn, ssm_a_re, ssm_a_im, ssm_log_step, ssm_b_re, ssm_b_im, ssm_c_re, ssm_c_im, ssm_d, w_glu, w_out, norm_mlp, w_mlp_in, w_mlp_out, norm_ple, w_ple_gate, w_ple, norm_final):
    depth = w_in.shape[0]
    d_model = x_prompt.shape[-1]
    layers = []
    for i in range(depth):
        ssm_w = (ssm_a_re[i], ssm_a_im[i], ssm_log_step[i], ssm_b_re[i], ssm_b_im[i], ssm_c_re[i], ssm_c_im[i], ssm_d[i])
        layers.append(_layer_weights(w_in[i], norm_q[i], w_uq[i], norm_kv[i], w_uk[i], w_uv[i], w_o_attn[i],
                                     ssm_w, w_glu[i], d_model))
    outs = []
    for x, p in ((x_prompt, p_prompt), (x_sample, p_sample)):
        batch, seq_len, _ = x.shape
        h = x.reshape(batch * seq_len, d_model)
        for i in range(depth):
            final = i == depth - 1
            h = _encoder_layer(h, p[i].reshape(batch * seq_len, -1), layers[i], norm_mix[i], norm_q[i], norm_kv[i],
                               w_out[i], norm_mlp[i], w_mlp_in[i], w_mlp_out[i], norm_ple[i], w_ple_gate[i], w_ple[i],
                               norm_final if final else norm_ple[i], final, batch, seq_len)
        outs.append(h.reshape(batch, seq_len, d_model))
    return tuple(outs)
```

```python
import functools
import math

import jax
import jax.numpy as jnp
from jax import lax
from jax.experimental import pallas as pl
from jax.experimental.pallas import tpu as pltpu

NORM_EPS = 1e-6
ROPE_THETA = 10000.0
LANES = 128
SUBLANES = 8
BF16_ROWS = 16
FLASH_TQ = 2048
FLASH_TK = 512
FLASH_UNROLL = 4
MXU_DIM = 256
VMEM_LIMIT = 56 * 1024 * 1024
SSM_CHUNK = 32
BF16 = jnp.bfloat16
F32 = jnp.float32


def _params(n_axes, n_parallel=None):
    n_parallel = n_axes if n_parallel is None else n_parallel
    sem = ("parallel",) * n_parallel + ("arbitrary",) * (n_axes - n_parallel)
    return pltpu.CompilerParams(dimension_semantics=sem, vmem_limit_bytes=VMEM_LIMIT)


def _resident(shape):
    nd = len(shape)
    return pl.BlockSpec(shape, lambda *_: (0,) * nd, pipeline_mode=pl.Buffered(1))


def _rms(x, g):
    r = lax.rsqrt(jnp.mean(x * x, axis=-1, keepdims=True) + NORM_EPS)
    return (x * r) * g


def _dot(a, b):
    return jnp.dot(a, b, preferred_element_type=F32)


def _dot_nt(a, b):
    return lax.dot_general(a, b, (((1,), (1,)), ((), ())), preferred_element_type=F32)


def _in_proj_kernel(x_ref, g_ref, wq_ref, wkv_ref, wkr_ref, wu_ref, gq_ref, gkv_ref, cos_ref, sin_ref,
                    xn_ref, qn_ref, ckv_ref, kr_ref, u_ref):
    xn = _rms(x_ref[...], g_ref[...]).astype(BF16)
    xn_ref[...] = xn
    qn_ref[...] = _rms(_dot(xn, wq_ref[...]), gq_ref[...]).astype(BF16)
    ckv_ref[...] = _rms(_dot(xn, wkv_ref[...]), gkv_ref[...]).astype(BF16)
    kr = _dot(xn, wkr_ref[...])
    kr_ref[...] = (kr * cos_ref[...] + pltpu.roll(kr, LANES // 2, 1) * sin_ref[...]).astype(BF16)
    u_ref[...] = _dot(xn, wu_ref[...]).astype(BF16)


def in_proj(x, g, wq, wkv, wkr, wu, gq, gkv, cos_l, sin_l, seq_len, tm):
    t, d = x.shape
    cq, ckv, sw = wq.shape[1], wkv.shape[1], wu.shape[1]
    npos = seq_len // tm
    row = lambda i: (i, 0)
    pos = lambda i: (i % npos, 0)
    widths = (d, cq, ckv, LANES, sw)
    return pl.pallas_call(
        _in_proj_kernel,
        grid=(t // tm,),
        in_specs=[pl.BlockSpec((tm, d), row), _resident((1, d)),
                  _resident((d, cq)), _resident((d, ckv)), _resident((d, LANES)), _resident((d, sw)),
                  _resident((1, cq)), _resident((1, ckv)),
                  pl.BlockSpec((tm, LANES), pos), pl.BlockSpec((tm, LANES), pos)],
        out_specs=[pl.BlockSpec((tm, w), row) for w in widths],
        out_shape=[jax.ShapeDtypeStruct((t, w), BF16) for w in widths],
        compiler_params=_params(1),
        name="in_proj",
    )(x, g.reshape(1, d), wq, wkv, wkr, wu, gq.reshape(1, cq), gkv.reshape(1, ckv), cos_l, sin_l)


def _attn_prep_kernel(qn_ref, ckv_ref, kr_ref, wuq_ref, wuk_ref, wuv_ref, cos_ref, sin_ref,
                      qt_ref, k_ref, vt_ref, *, n_heads, dn, dv):
    qn = qn_ref[...]
    ckv = ckv_ref[...]
    cos_t = cos_ref[...]
    sin_t = sin_ref[...]
    half = LANES // 2
    for h in range(n_heads):
        q = _dot_nt(wuq_ref[h], qn)
        qr = q[dn:]
        qr_sw = jnp.concatenate([qr[half:], qr[:half]], axis=0)
        qt_ref[0, h, :dn, :] = q[:dn].astype(BF16)
        qt_ref[0, h, dn:, :] = (qr * cos_t + qr_sw * sin_t).astype(BF16)
    k_all = _dot(ckv, wuk_ref[...])
    kr = kr_ref[...]
    for h in range(n_heads):
        k_ref[0, h, :, :dn] = k_all[:, h * dn:(h + 1) * dn].astype(BF16)
        k_ref[0, h, :, dn:] = kr
    vt = _dot_nt(wuv_ref[...], ckv).astype(BF16)
    tm = vt.shape[1]
    ones_rows = (lax.broadcasted_iota(jnp.int32, (BF16_ROWS, tm), 0) == 0).astype(BF16)
    dve = dv + BF16_ROWS
    for h in range(n_heads):
        vt_ref[0, h * dve:h * dve + dv, :] = vt[h * dv:(h + 1) * dv]
        vt_ref[0, h * dve + dv:(h + 1) * dve, :] = ones_rows


def attn_prep(qn, ckv, kr, wuq_t, wuk, wuv_t, cos_t, sin_t, batch, seq_len, tm):
    n_heads, dk, cq = wuq_t.shape
    dn = dk - LANES
    dv = wuv_t.shape[0] // n_heads
    hdv = n_heads * (dv + BF16_ROWS)
    ckv_dim = ckv.shape[1]
    npos = seq_len // tm
    row = lambda b, i: (b * npos + i, 0)
    col = lambda b, i: (0, i)
    return pl.pallas_call(
        functools.partial(_attn_prep_kernel, n_heads=n_heads, dn=dn, dv=dv),
        grid=(batch, npos),
        in_specs=[pl.BlockSpec((tm, cq), row), pl.BlockSpec((tm, ckv_dim), row), pl.BlockSpec((tm, LANES), row),
                  _resident((n_heads, dk, cq)), _resident((ckv_dim, n_heads * dn)), _resident(wuv_t.shape),
                  pl.BlockSpec((LANES, tm), col), pl.BlockSpec((LANES, tm), col)],
        out_specs=[pl.BlockSpec((1, n_heads, dk, tm), lambda b, i: (b, 0, 0, i)),
                   pl.BlockSpec((1, n_heads, tm, dk), lambda b, i: (b, 0, i, 0)),
                   pl.BlockSpec((1, hdv, tm), lambda b, i: (b, 0, i))],
        out_shape=[jax.ShapeDtypeStruct((batch, n_heads, dk, seq_len), BF16),
                   jax.ShapeDtypeStruct((batch, n_heads, seq_len, dk), BF16),
                   jax.ShapeDtypeStruct((batch, hdv, seq_len), BF16)],
        compiler_params=_params(2),
        name="attn_prep",
    )(qn, ckv, kr, wuq_t, wuk, wuv_t, cos_t, sin_t)


def _flash_kernel(qt_ref, k_ref, vt_ref, o_ref, m_ref, acc_ref, s0_ref, s1_ref, c0_ref, c1_ref,
                  *, tk, nk, dv, unroll):
    qt = qt_ref[0, 0]
    tq = qt.shape[1]
    m_ref[...] = jnp.full(m_ref.shape, -jnp.inf, F32)
    acc_ref[...] = jnp.zeros(acc_ref.shape, F32)

    col_blocks = [slice(c, c + MXU_DIM) for c in range(0, tq, MXU_DIM)]

    def scores(j, buf, cols):
        s_ref, c_ref = buf
        off = pl.multiple_of(j * tk, tk)
        s = _dot(k_ref[0, 0, pl.ds(off, tk), :], qt[:, cols])
        s_ref[:, cols] = s
        c_ref[:, cols] = jnp.max(s, axis=0, keepdims=True)

    def consume(j, buf, cols):
        s_ref, c_ref = buf
        off = pl.multiple_of(j * tk, tk)
        m_prev = m_ref[:, cols]
        m_new = jnp.maximum(m_prev, c_ref[:, cols])
        alpha = jnp.exp2(m_prev - m_new)
        p = jnp.exp2(s_ref[:, cols] - m_new).astype(BF16)
        acc_ref[:, cols] = alpha * acc_ref[:, cols] + _dot(vt_ref[0, :, pl.ds(off, tk)], p)
        m_ref[:, cols] = m_new

    bufs = ((s0_ref, c0_ref), (s1_ref, c1_ref))
    for cols in col_blocks:
        scores(0, bufs[0], cols)

    def group(base, last):
        for u in range(unroll):
            for cols in col_blocks:
                if not (last and u == unroll - 1):
                    scores(base + u + 1, bufs[(u + 1) % 2], cols)
                consume(base + u, bufs[u % 2], cols)

    def body(i, carry):
        group(i * unroll, False)
        return carry

    lax.fori_loop(0, nk // unroll - 1, body, 0)
    group(nk - unroll, True)
    acc = acc_ref[...]
    o_ref[0] = (acc[:dv] / acc[dv:dv + 1]).T.astype(BF16)


def flash_attn(qt, k, vt, dv, tq, tk):
    batch, n_heads, dk, seq_len = qt.shape
    dve = vt.shape[1] // n_heads
    nk = seq_len // tk
    unroll = min(FLASH_UNROLL, nk)
    assert unroll % 2 == 0 and nk % unroll == 0
    return pl.pallas_call(
        functools.partial(_flash_kernel, tk=tk, nk=nk, dv=dv, unroll=unroll),
        grid=(batch, n_heads, seq_len // tq),
        in_specs=[pl.BlockSpec((1, 1, dk, tq), lambda b, h, i: (b, h, 0, i)),
                  pl.BlockSpec((1, 1, seq_len, dk), lambda b, h, i: (b, h, 0, 0)),
                  pl.BlockSpec((1, dve, seq_len), lambda b, h, i: (b, h, 0))],
        out_specs=pl.BlockSpec((1, tq, dv), lambda b, h, i: (b, i, h)),
        out_shape=jax.ShapeDtypeStruct((batch, seq_len, n_heads * dv), BF16),
        scratch_shapes=[pltpu.VMEM((1, tq), F32), pltpu.VMEM((dve, tq), F32),
                        pltpu.VMEM((tk, tq), F32), pltpu.VMEM((tk, tq), F32),
                        pltpu.VMEM((1, tq), F32), pltpu.VMEM((1, tq), F32)],
        compiler_params=_params(3),
        name="flash_attn",
    )(qt, k, vt)


def _rows_to_block(rows):
    n = rows[0].shape[1]
    rid = lax.broadcasted_iota(jnp.int32, (SUBLANES, n), 0)
    blk = jnp.broadcast_to(rows[-1], (SUBLANES, n))
    for k in range(SUBLANES - 2, -1, -1):
        blk = jnp.where(rid == k, rows[k], blk)
    return blk


def _ssm_kernel(u_ref, toep_ref, w1_ref, w2_ref, dec_ref, y_ref, s_ref, xf_ref, xb_ref, *, n_seq, n_chunk, n_state):
    u = u_ref[0]
    s_ref[...] = _dot(u, w1_ref[0])
    dec = dec_ref[0]
    a_re, a_im = dec[0:1], dec[1:2]
    fwd_lane = lax.broadcasted_iota(jnp.int32, (1, LANES), 1) < n_state
    n_blk = n_chunk // SUBLANES

    def step(i, carry):
        new = []
        for q in range(n_seq):
            x_re, x_im = carry[2 * q], carry[2 * q + 1]
            rf = pl.multiple_of(q * n_chunk + i * SUBLANES, SUBLANES)
            rb = pl.multiple_of(q * n_chunk + (n_blk - 1 - i) * SUBLANES, SUBLANES)
            sf = s_ref[pl.ds(rf, SUBLANES), :]
            sb = s_ref[pl.ds(rb, SUBLANES), :]
            rows_re, rows_im = [], []
            for k in range(SUBLANES):
                kb = SUBLANES - 1 - k
                rows_re.append(x_re)
                rows_im.append(x_im)
                s_re = jnp.where(fwd_lane, sf[k:k + 1, :LANES], sb[kb:kb + 1, :LANES])
                s_im = jnp.where(fwd_lane, sf[k:k + 1, LANES:], sb[kb:kb + 1, LANES:])
                x_re, x_im = a_re * x_re - a_im * x_im + s_re, a_re * x_im + a_im * x_re + s_im
            xf_ref[pl.ds(rf, SUBLANES), :LANES] = _rows_to_block(rows_re)
            xf_ref[pl.ds(rf, SUBLANES), LANES:] = _rows_to_block(rows_im)
            xb_ref[pl.ds(rb, SUBLANES), :LANES] = _rows_to_block(rows_re[::-1])
            xb_ref[pl.ds(rb, SUBLANES), LANES:] = _rows_to_block(rows_im[::-1])
            new += [x_re, x_im]
        return tuple(new)

    zero = jnp.zeros((1, LANES), F32)
    lax.fori_loop(0, n_blk, step, (zero,) * (2 * n_seq))
    y_ref[0] = (_dot(u, toep_ref[0]) + _dot_nt(xf_ref[...].astype(BF16), w2_ref[0, 0])
                + _dot_nt(xb_ref[...].astype(BF16), w2_ref[0, 1])).astype(y_ref.dtype)


def ssm(u_rows, toep, w1, w2, dec, n_seq, n_chunk):
    groups, rows, tp = u_rows.shape
    n_state = w1.shape[2] // 4
    assert 2 * n_state == LANES and n_chunk % SUBLANES == 0
    g3 = lambda g: (g, 0, 0)
    return pl.pallas_call(
        functools.partial(_ssm_kernel, n_seq=n_seq, n_chunk=n_chunk, n_state=n_state),
        grid=(groups,),
        in_specs=[pl.BlockSpec((1, rows, tp), g3), pl.BlockSpec((1, tp, tp), g3),
                  pl.BlockSpec((1, tp, 4 * n_state), g3), pl.BlockSpec((1, 2, tp, 4 * n_state), lambda g: (g, 0, 0, 0)),
                  pl.BlockSpec((1, 2, LANES), g3)],
        out_specs=pl.BlockSpec((1, rows, tp), g3),
        out_shape=jax.ShapeDtypeStruct((groups, rows, tp), BF16),
        scratch_shapes=[pltpu.VMEM((rows, 4 * n_state), F32)] * 3,
        compiler_params=_params(1),
        name="ssm",
    )(u_rows, toep, w1, w2, dec)


def ssm_operators(a_re, a_im, log_step, b_re, b_im, c_re, c_im, d_skip, chunk):
    _, groups, n_state = a_re.shape
    p = b_re.shape[-1]
    step = jnp.exp(log_step)[..., None]
    mag = jnp.exp(step * a_re)
    ab_re, ab_im = mag * jnp.cos(step * a_im), mag * jnp.sin(step * a_im)
    den = a_re * a_re + a_im * a_im
    nr, ni = ab_re - 1.0, ab_im
    f_re = (nr * a_re + ni * a_im) / den
    f_im = (ni * a_re - nr * a_im) / den
    bb_re = f_re[..., None] * b_re - f_im[..., None] * b_im
    bb_im = f_re[..., None] * b_im + f_im[..., None] * b_re
    j = jnp.arange(chunk + 1, dtype=F32)[:, None, None, None]
    pw_mag = jnp.exp(j * (step * a_re)[None])
    pw_re = pw_mag * jnp.cos(j * (step * a_im)[None])
    pw_im = pw_mag * jnp.sin(j * (step * a_im)[None])
    def fb(x_f, x_b):
        return jnp.concatenate([x_f, x_b], axis=-1).transpose(1, 0, 2)

    def re_im(f):
        return jnp.stack([f(pw_re), f(pw_im)], axis=1)

    zero = jnp.zeros((chunk - 1, groups, n_state), F32)
    pw_lag = re_im(lambda w: fb(jnp.concatenate([zero, w[:chunk, 0]]), jnp.concatenate([w[chunk - 1::-1, 1], zero])))
    pw_lag = jnp.pad(pw_lag, ((0, 0), (0, 0), (0, 1), (0, 0)))
    pw_in = re_im(lambda w: fb(w[chunk - 1::-1, 0], w[:chunk, 1]))
    pw_out = re_im(lambda w: fb(w[1:chunk + 1, 0], w[chunk:0:-1, 1]))
    dec = jnp.stack([fb(pw_re[chunk:, 0], pw_re[chunk:, 1])[:, 0], fb(pw_im[chunk:, 0], pw_im[chunk:, 1])[:, 0]],
                    axis=1)
    bt = jnp.stack([jnp.concatenate([bb_re[0], bb_re[1]], axis=1), jnp.concatenate([bb_im[0], bb_im[1]], axis=1)],
                   axis=1).transpose(0, 1, 3, 2)
    cc = jnp.stack([jnp.concatenate([c_re[0], c_re[1]], axis=-1), jnp.concatenate([c_im[0], c_im[1]], axis=-1)],
                   axis=1)
    eye = jnp.eye(p, LANES, dtype=F32)
    skip = d_skip.reshape(groups, p)[:, :, None] * eye
    n_rows = -(-(2 * chunk - 1) * p // LANES) * LANES
    sel = jnp.zeros((n_rows, LANES), F32).at[(chunk - 1) * p:chunk * p].set(eye)
    toep, w1, w2 = ssm_tables(bt, cc, skip, sel.astype(BF16), pw_lag, pw_in, pw_out, chunk)
    return toep, w1, w2, dec


def _ssm_tables_kernel(bt_ref, c_ref, skip_ref, sel_ref, pwl_ref, pwi_ref, pwo_ref, toep_ref, w1_ref, w2_ref, rt_ref,
                       *, chunk, n_state):
    p = bt_ref.shape[2]
    bt_re, bt_im = bt_ref[0, 0], bt_ref[0, 1]
    c_re, c_im = c_ref[0, 0], c_ref[0, 1]
    rt_ref[...] = jnp.zeros(rt_ref.shape, BF16)
    for li in range(2 * chunk - 1):
        a_r, a_i = pwl_ref[0, 0, li:li + 1, :], pwl_ref[0, 1, li:li + 1, :]
        rt_ref[li * p:(li + 1) * p, :LANES] = (c_re * a_r - c_im * a_i).astype(BF16)
        rt_ref[li * p:(li + 1) * p, LANES:] = (c_re * a_i + c_im * a_r).astype(BF16)
    lhs = jnp.concatenate([bt_re, -bt_im], axis=1).astype(BF16)
    by_lag = _dot_nt(lhs, rt_ref[...]) + _dot_nt(skip_ref[0].astype(BF16), sel_ref[...])
    width = by_lag.shape[1]
    for s in range(chunk):
        off = (chunk - 1 - s) * p
        win = by_lag if off == 0 else pltpu.roll(by_lag, width - off, 1)
        toep_ref[0, s * p:(s + 1) * p, :] = win[:, :chunk * p].astype(BF16)
    fwd = lax.broadcasted_iota(jnp.int32, (p, LANES), 1) < n_state
    for s in range(chunk):
        rows = slice(s * p, (s + 1) * p)
        a_r, a_i = pwi_ref[0, 0, s:s + 1, :], pwi_ref[0, 1, s:s + 1, :]
        w1_ref[0, rows, :LANES] = (bt_re * a_r - bt_im * a_i).astype(BF16)
        w1_ref[0, rows, LANES:] = (bt_re * a_i + bt_im * a_r).astype(BF16)
        a_r, a_i = pwo_ref[0, 0, s:s + 1, :], pwo_ref[0, 1, s:s + 1, :]
        o_re = c_re * a_r - c_im * a_i
        o_im = -(c_re * a_i + c_im * a_r)
        w2_ref[0, 0, rows, :LANES] = jnp.where(fwd, o_re, 0.0).astype(BF16)
        w2_ref[0, 0, rows, LANES:] = jnp.where(fwd, o_im, 0.0).astype(BF16)
        w2_ref[0, 1, rows, :LANES] = jnp.where(fwd, 0.0, o_re).astype(BF16)
        w2_ref[0, 1, rows, LANES:] = jnp.where(fwd, 0.0, o_im).astype(BF16)


def ssm_tables(bt, cc, skip, sel, pw_lag, pw_in, pw_out, chunk):
    groups, _, p, lanes = bt.shape
    n_state = lanes // 2
    tp = chunk * p
    g4 = lambda g: (g, 0, 0, 0)
    blk4 = lambda a: pl.BlockSpec((1,) + a.shape[1:], g4)
    return pl.pallas_call(
        functools.partial(_ssm_tables_kernel, chunk=chunk, n_state=n_state),
        grid=(groups,),
        in_specs=[blk4(bt), blk4(cc), pl.BlockSpec((1, p, LANES), lambda g: (g, 0, 0)), _resident(sel.shape),
                  blk4(pw_lag), blk4(pw_in), blk4(pw_out)],
        out_specs=[pl.BlockSpec((1, tp, tp), lambda g: (g, 0, 0)), pl.BlockSpec((1, tp, 2 * lanes), lambda g: (g, 0, 0)),
                   pl.BlockSpec((1, 2, tp, 2 * lanes), g4)],
        out_shape=[jax.ShapeDtypeStruct((groups, tp, tp), BF16), jax.ShapeDtypeStruct((groups, tp, 2 * lanes), BF16),
                   jax.ShapeDtypeStruct((groups, 2, tp, 2 * lanes), BF16)],
        scratch_shapes=[pltpu.VMEM(sel.shape[:1] + (2 * lanes,), BF16)],
        compiler_params=_params(1),
        name="ssm_tables",
    )(bt, cc, skip, sel, pw_lag, pw_in, pw_out)


def _mix_kernel(xn_ref, o_ref, y_ref, wga_ref, wgs_ref, wo_ref, wz1_ref, wz2_ref, mix_ref, gy_ref):
    @pl.when(pl.program_id(1) == 0)
    def _():
        gy_ref[...] = jax.nn.gelu(y_ref[...].astype(F32)).astype(BF16)

    xn = xn_ref[...]
    gy = gy_ref[...]
    o_a = _dot(o_ref[...], wo_ref[...])
    o_s = _dot(gy, wz1_ref[...]) * jax.nn.sigmoid(_dot(gy, wz2_ref[...]))
    g_a = jax.nn.sigmoid(_dot(xn, wga_ref[...]))
    g_s = jax.nn.sigmoid(_dot(xn, wgs_ref[...]))
    mix_ref[...] = (g_a * o_a + g_s * o_s).astype(BF16)


def mix_merge(xn, o, y, wga, wgs, wo, wz1, wz2, tm, tn):
    t, d = xn.shape
    ho, sw = o.shape[1], y.shape[1]
    row = lambda i, j: (i, 0)
    col = lambda i, j: (0, j)
    return pl.pallas_call(
        _mix_kernel,
        grid=(t // tm, d // tn),
        in_specs=[pl.BlockSpec((tm, d), row), pl.BlockSpec((tm, ho), row), pl.BlockSpec((tm, sw), row),
                  pl.BlockSpec((d, tn), col), pl.BlockSpec((d, tn), col), pl.BlockSpec((ho, tn), col),
                  pl.BlockSpec((sw, tn), col), pl.BlockSpec((sw, tn), col)],
        out_specs=pl.BlockSpec((tm, tn), lambda i, j: (i, j)),
        out_shape=jax.ShapeDtypeStruct((t, d), BF16),
        scratch_shapes=[pltpu.VMEM((tm, sw), BF16)],
        compiler_params=_params(2, 1),
        name="mix_merge",
    )(xn, o, y, wga, wgs, wo, wz1, wz2)


def _resid_norm_kernel(x_ref, mix_ref, w_ref, g_ref, h_ref, n_ref):
    h = x_ref[...] + _dot(mix_ref[...], w_ref[...])
    h_ref[...] = h
    n_ref[...] = _rms(h, g_ref[...]).astype(BF16)


def resid_norm(x, mix, w, g, tm):
    t, d = x.shape
    row = lambda i: (i, 0)
    return pl.pallas_call(
        _resid_norm_kernel,
        grid=(t // tm,),
        in_specs=[pl.BlockSpec((tm, d), row), pl.BlockSpec((tm, d), row), _resident((d, d)), _resident((1, d))],
        out_specs=[pl.BlockSpec((tm, d), row), pl.BlockSpec((tm, d), row)],
        out_shape=[jax.ShapeDtypeStruct((t, d), F32), jax.ShapeDtypeStruct((t, d), BF16)],
        compiler_params=_params(1),
        name="resid_norm",
    )(x, mix, w, g.reshape(1, d))


def _mlp_kernel(n_ref, h_ref, w1_ref, w2_ref, o_ref, acc_ref):
    j = pl.program_id(1)

    @pl.when(j == 0)
    def _():
        acc_ref[...] = h_ref[...]

    a = jnp.maximum(_dot(n_ref[...], w1_ref[...]), 0.0)
    acc_ref[...] += _dot((a * a).astype(BF16), w2_ref[...])

    @pl.when(j == pl.num_programs(1) - 1)
    def _():
        o_ref[...] = acc_ref[...]


def mlp(n, h, w1, w2, tm, tf):
    t, d = h.shape
    ff = w1.shape[1]
    row = lambda i, j: (i, 0)
    return pl.pallas_call(
        _mlp_kernel,
        grid=(t // tm, ff // tf),
        in_specs=[pl.BlockSpec((tm, d), row), pl.BlockSpec((tm, d), row),
                  pl.BlockSpec((d, tf), lambda i, j: (0, j)), pl.BlockSpec((tf, d), lambda i, j: (j, 0))],
        out_specs=pl.BlockSpec((tm, d), row),
        out_shape=jax.ShapeDtypeStruct((t, d), F32),
        scratch_shapes=[pltpu.VMEM((tm, d), F32)],
        compiler_params=_params(2, 1),
        name="mlp",
    )(n, h, w1, w2)


def _ple_kernel(h_ref, p_ref, wg_ref, wp_ref, gp_ref, gf_ref, o_ref, *, final):
    h = h_ref[...]
    gate = jax.nn.sigmoid(_dot(_rms(h, gp_ref[...]).astype(BF16), wg_ref[...]))
    h = h + gate * _dot(p_ref[...].astype(BF16), wp_ref[...])
    o_ref[...] = _rms(h, gf_ref[...]) if final else h


def ple(h, p, wg, wp, gp, gf, tm, final):
    t, d = h.shape
    pd = p.shape[1]
    row = lambda i: (i, 0)
    return pl.pallas_call(
        functools.partial(_ple_kernel, final=final),
        grid=(t // tm,),
        in_specs=[pl.BlockSpec((tm, d), row), pl.BlockSpec((tm, pd), row), _resident((d, d)), _resident((pd, d)),
                  _resident((1, d)), _resident((1, d))],
        out_specs=pl.BlockSpec((tm, d), row),
        out_shape=jax.ShapeDtypeStruct((t, d), F32),
        compiler_params=_params(1),
        name="ple",
    )(h, p, wg, wp, gp.reshape(1, d), gf.reshape(1, d))


def _rope_tables(length, rope_dim):
    pos = jnp.arange(length, dtype=F32)
    inv = ROPE_THETA ** (-jnp.arange(0, rope_dim, 2, dtype=F32) / rope_dim)
    ang = pos[:, None] * inv[None, :]
    return jnp.cos(ang), jnp.sin(ang)


def _rope_lanes(cos, sin):
    length, half = cos.shape
    z = jnp.zeros((length, LANES // 2 - half), F32)
    return (jnp.concatenate([cos, z, cos, z], axis=1), jnp.concatenate([-sin, z, sin, z], axis=1))


def _spread_rope(w, half):
    z = jnp.zeros(w.shape[:-1] + (LANES // 2 - half,), w.dtype)
    return jnp.concatenate([w[..., :half], z, w[..., half:], z], axis=-1)


def _tile(n, pref):
    t = min(n, pref)
    while n % t:
        t //= 2
    return t


def _layer_weights(w_in, norm_q, w_uq, norm_kv, w_uk, w_uv, w_o_attn, ssm, w_glu, d_model):
    cq, ckv = norm_q.shape[0], norm_kv.shape[0]
    n_heads, dn = w_uk.shape[1], w_uk.shape[2]
    dv = w_uv.shape[2]
    sw = ssm[7].shape[0]
    rope = w_in.shape[1] - (cq + ckv + sw + 2 * d_model)
    half = rope // 2
    o_q, o_kv, o_kr, o_u, o_ga, o_gs = 0, cq, cq + ckv, cq + ckv + rope, cq + ckv + rope + sw, cq + ckv + rope + sw + d_model
    scale = (dn + rope) ** -0.5 * math.log2(math.e)
    wuq = w_uq.reshape(cq, n_heads, dn + rope) * scale
    wuq_t = jnp.concatenate([wuq[..., :dn], _spread_rope(wuq[..., dn:], half)], axis=-1).transpose(1, 2, 0)
    return dict(
        rope=rope, n_heads=n_heads, dn=dn, dv=dv,
        wq=w_in[:, o_q:o_kv].astype(BF16), wkv=w_in[:, o_kv:o_kr].astype(BF16),
        wkr=_spread_rope(w_in[:, o_kr:o_u], half).astype(BF16),
        wu=w_in[:, o_u:o_ga].astype(BF16), wga=w_in[:, o_ga:o_gs].astype(BF16), wgs=w_in[:, o_gs:].astype(BF16),
        wuq_t=wuq_t.astype(BF16),
        wuk=w_uk.reshape(ckv, n_heads * dn).astype(BF16),
        wuv_t=w_uv.reshape(ckv, n_heads * dv).T.astype(BF16),
        wo=w_o_attn.astype(BF16),
        wz1=w_glu[:, :d_model].astype(BF16), wz2=w_glu[:, d_model:].astype(BF16),
        ssm_ops=ssm_operators(*ssm, SSM_CHUNK),
    )


def _encoder_layer(h, p, lw, norm_mix, norm_q, norm_kv, w_out, norm_mlp, w_mlp_in, w_mlp_out,
                   norm_ple, w_ple_gate, w_ple, norm_out, final, batch, seq_len):
    t, d = h.shape
    tm = _tile(seq_len, 512)
    cos, sin = _rope_tables(seq_len, lw['rope'])
    cos_l, sin_l = _rope_lanes(cos, sin)
    xn, qn, ckv, kr, u = in_proj(h, norm_mix, lw['wq'], lw['wkv'], lw['wkr'], lw['wu'], norm_q, norm_kv,
                                 cos_l, sin_l, seq_len, tm)
    qt, k, vt = attn_prep(qn, ckv, kr, lw['wuq_t'], lw['wuk'], lw['wuv_t'], cos_l.T, sin_l.T, batch, seq_len, tm)
    o = flash_attn(qt, k, vt, lw['dv'], _tile(seq_len, FLASH_TQ), _tile(seq_len, FLASH_TK)).reshape(t, -1)
    toep, w1, w2, dec = lw['ssm_ops']
    groups = toep.shape[0]
    gp = lw['wu'].shape[1] // groups
    n_chunk = seq_len // SSM_CHUNK
    u_rows = (u.reshape(batch * n_chunk, SSM_CHUNK, groups, gp).transpose(2, 0, 1, 3)
              .reshape(groups, batch * n_chunk, SSM_CHUNK * gp))
    y_rows = ssm(u_rows, toep, w1, w2, dec, batch, n_chunk)
    y = (y_rows.reshape(groups, batch * n_chunk, SSM_CHUNK, gp).transpose(1, 2, 0, 3).reshape(t, groups * gp))
    mix = mix_merge(xn, o, y, lw['wga'], lw['wgs'], lw['wo'], lw['wz1'], lw['wz2'], tm, _tile(d, 512))
    h1, n1 = resid_norm(h, mix, w_out.astype(BF16), norm_mlp, tm)
    h2 = mlp(n1, h1, w_mlp_in.astype(BF16), w_mlp_out.astype(BF16), tm, _tile(w_mlp_in.shape[1], 1024))
    return ple(h2, p, w_ple_gate.astype(BF16), w_ple.astype(BF16), norm_ple, norm_out, tm, final)


def kernel(x_prompt, x_sample, p_prompt, p_sample, norm_mix, w_in, norm_q, w_uq, norm_kv, w_uk, w_uv, w_o_attn, ssm_a_re, ssm_a_im, ssm_log_step, ssm_b_re, ssm_b_im, ssm_c_re, ssm_c_im, ssm_d, w_glu, w_out, norm_mlp, w_mlp_in, w_mlp_out, norm_ple, w_ple_gate, w_ple, norm_final):
    depth = w_in.shape[0]
    d_model = x_prompt.shape[-1]
    layers = []
    for i in range(depth):
        ssm_w = (ssm_a_re[i], ssm_a_im[i], ssm_log_step[i], ssm_b_re[i], ssm_b_im[i], ssm_c_re[i], ssm_c_im[i], ssm_d[i])
        layers.append(_layer_weights(w_in[i], norm_q[i], w_uq[i], norm_kv[i], w_uk[i], w_uv[i], w_o_attn[i],
                                     ssm_w, w_glu[i], d_model))
    outs = []
    for x, p in ((x_prompt, p_prompt), (x_sample, p_sample)):
        batch, seq_len, _ = x.shape
        h = x.reshape(batch * seq_len, d_model)
        for i in range(depth):
            final = i == depth - 1
            h = _encoder_layer(h, p[i].reshape(batch * seq_len, -1), layers[i], norm_mix[i], norm_q[i], norm_kv[i],
                               w_out[i], norm_mlp[i], w_mlp_in[i], w_mlp_out[i], norm_ple[i], w_ple_gate[i], w_ple[i],
                               norm_final if final else norm_ple[i], final, batch, seq_len)
        outs.append(h.reshape(batch, seq_len, d_model))
    return tuple(outs)
```

```python
import functools
import math

import jax
import jax.numpy as jnp
from jax import lax
from jax.experimental import pallas as pl
from jax.experimental.pallas import tpu as pltpu

NORM_EPS = 1e-6
ROPE_THETA = 10000.0
LANES = 128
SUBLANES = 8
BF16_ROWS = 16
FLASH_TQ = 2048
FLASH_TK = 512
FLASH_UNROLL = 4
MXU_DIM = 256
VMEM_LIMIT = 56 * 1024 * 1024
SSM_CHUNK = 32
BF16 = jnp.bfloat16
F32 = jnp.float32


def _params(n_axes, n_parallel=None):
    n_parallel = n_axes if n_parallel is None else n_parallel
    sem = ("parallel",) * n_parallel + ("arbitrary",) * (n_axes - n_parallel)
    return pltpu.CompilerParams(dimension_semantics=sem, vmem_limit_bytes=VMEM_LIMIT)


def _resident(shape):
    nd = len(shape)
    return pl.BlockSpec(shape, lambda *_: (0,) * nd, pipeline_mode=pl.Buffered(1))


def _rms(x, g):
    r = lax.rsqrt(jnp.mean(x * x, axis=-1, keepdims=True) + NORM_EPS)
    return (x * r) * g


def _dot(a, b):
    return jnp.dot(a, b, preferred_element_type=F32)


def _dot_nt(a, b):
    return lax.dot_general(a, b, (((1,), (1,)), ((), ())), preferred_element_type=F32)


def _in_proj_kernel(x_ref, g_ref, wq_ref, wkv_ref, wkr_ref, wu_ref, gq_ref, gkv_ref, cos_ref, sin_ref,
                    xn_ref, qn_ref, ckv_ref, kr_ref, u_ref):
    xn = _rms(x_ref[...], g_ref[...]).astype(BF16)
    xn_ref[...] = xn
    qn_ref[...] = _rms(_dot(xn, wq_ref[...]), gq_ref[...]).astype(BF16)
    ckv_ref[...] = _rms(_dot(xn, wkv_ref[...]), gkv_ref[...]).astype(BF16)
    kr = _dot(xn, wkr_ref[...])
    kr_ref[...] = (kr * cos_ref[...] + pltpu.roll(kr, LANES // 2, 1) * sin_ref[...]).astype(BF16)
    u_ref[...] = _dot(xn, wu_ref[...])


def in_proj(x, g, wq, wkv, wkr, wu, gq, gkv, cos_l, sin_l, seq_len, tm):
    t, d = x.shape
    cq, ckv, sw = wq.shape[1], wkv.shape[1], wu.shape[1]
    npos = seq_len // tm
    row = lambda i: (i, 0)
    pos = lambda i: (i % npos, 0)
    widths = (d, cq, ckv, LANES, sw)
    return pl.pallas_call(
        _in_proj_kernel,
        grid=(t // tm,),
        in_specs=[pl.BlockSpec((tm, d), row), _resident((1, d)),
                  _resident((d, cq)), _resident((d, ckv)), _resident((d, LANES)), _resident((d, sw)),
                  _resident((1, cq)), _resident((1, ckv)),
                  pl.BlockSpec((tm, LANES), pos), pl.BlockSpec((tm, LANES), pos)],
        out_specs=[pl.BlockSpec((tm, w), row) for w in widths],
        out_shape=[jax.ShapeDtypeStruct((t, w), BF16) for w in widths[:-1]] + [jax.ShapeDtypeStruct((t, sw), F32)],
        compiler_params=_params(1),
        name="in_proj",
    )(x, g.reshape(1, d), wq, wkv, wkr, wu, gq.reshape(1, cq), gkv.reshape(1, ckv), cos_l, sin_l)


def _attn_prep_kernel(qn_ref, ckv_ref, kr_ref, wuq_ref, wuk_ref, wuv_ref, cos_ref, sin_ref,
                      qt_ref, k_ref, vt_ref, *, n_heads, dn, dv):
    qn = qn_ref[...]
    ckv = ckv_ref[...]
    cos_t = cos_ref[...]
    sin_t = sin_ref[...]
    half = LANES // 2
    for h in range(n_heads):
        q = _dot_nt(wuq_ref[h], qn)
        qr = q[dn:]
        qr_sw = jnp.concatenate([qr[half:], qr[:half]], axis=0)
        qt_ref[0, h, :dn, :] = q[:dn].astype(BF16)
        qt_ref[0, h, dn:, :] = (qr * cos_t + qr_sw * sin_t).astype(BF16)
    k_all = _dot(ckv, wuk_ref[...])
    kr = kr_ref[...]
    for h in range(n_heads):
        k_ref[0, h, :, :dn] = k_all[:, h * dn:(h + 1) * dn].astype(BF16)
        k_ref[0, h, :, dn:] = kr
    vt = _dot_nt(wuv_ref[...], ckv).astype(BF16)
    tm = vt.shape[1]
    ones_rows = (lax.broadcasted_iota(jnp.int32, (BF16_ROWS, tm), 0) == 0).astype(BF16)
    dve = dv + BF16_ROWS
    for h in range(n_heads):
        vt_ref[0, h * dve:h * dve + dv, :] = vt[h * dv:(h + 1) * dv]
        vt_ref[0, h * dve + dv:(h + 1) * dve, :] = ones_rows


def attn_prep(qn, ckv, kr, wuq_t, wuk, wuv_t, cos_t, sin_t, batch, seq_len, tm):
    n_heads, dk, cq = wuq_t.shape
    dn = dk - LANES
    dv = wuv_t.shape[0] // n_heads
    hdv = n_heads * (dv + BF16_ROWS)
    ckv_dim = ckv.shape[1]
    npos = seq_len // tm
    row = lambda b, i: (b * npos + i, 0)
    col = lambda b, i: (0, i)
    return pl.pallas_call(
        functools.partial(_attn_prep_kernel, n_heads=n_heads, dn=dn, dv=dv),
        grid=(batch, npos),
        in_specs=[pl.BlockSpec((tm, cq), row), pl.BlockSpec((tm, ckv_dim), row), pl.BlockSpec((tm, LANES), row),
                  _resident((n_heads, dk, cq)), _resident((ckv_dim, n_heads * dn)), _resident(wuv_t.shape),
                  pl.BlockSpec((LANES, tm), col), pl.BlockSpec((LANES, tm), col)],
        out_specs=[pl.BlockSpec((1, n_heads, dk, tm), lambda b, i: (b, 0, 0, i)),
                   pl.BlockSpec((1, n_heads, tm, dk), lambda b, i: (b, 0, i, 0)),
                   pl.BlockSpec((1, hdv, tm), lambda b, i: (b, 0, i))],
        out_shape=[jax.ShapeDtypeStruct((batch, n_heads, dk, seq_len), BF16),
                   jax.ShapeDtypeStruct((batch, n_heads, seq_len, dk), BF16),
                   jax.ShapeDtypeStruct((batch, hdv, seq_len), BF16)],
        compiler_params=_params(2),
        name="attn_prep",
    )(qn, ckv, kr, wuq_t, wuk, wuv_t, cos_t, sin_t)


def _flash_kernel(qt_ref, k_ref, vt_ref, o_ref, m_ref, acc_ref, s0_ref, s1_ref, c0_ref, c1_ref,
                  *, tk, nk, dv, unroll):
    qt = qt_ref[0, 0]
    tq = qt.shape[1]
    m_ref[...] = jnp.full(m_ref.shape, -jnp.inf, F32)
    acc_ref[...] = jnp.zeros(acc_ref.shape, F32)

    col_blocks = [slice(c, c + MXU_DIM) for c in range(0, tq, MXU_DIM)]

    def scores(j, buf, cols):
        s_ref, c_ref = buf
        off = pl.multiple_of(j * tk, tk)
        s = _dot(k_ref[0, 0, pl.ds(off, tk), :], qt[:, cols])
        s_ref[:, cols] = s
        c_ref[:, cols] = jnp.max(s, axis=0, keepdims=True)

    def consume(j, buf, cols):
        s_ref, c_ref = buf
        off = pl.multiple_of(j * tk, tk)
        m_prev = m_ref[:, cols]
        m_new = jnp.maximum(m_prev, c_ref[:, cols])
        alpha = jnp.exp2(m_prev - m_new)
        p = jnp.exp2(s_ref[:, cols] - m_new).astype(BF16)
        acc_ref[:, cols] = alpha * acc_ref[:, cols] + _dot(vt_ref[0, :, pl.ds(off, tk)], p)
        m_ref[:, cols] = m_new

    bufs = ((s0_ref, c0_ref), (s1_ref, c1_ref))
    for cols in col_blocks:
        scores(0, bufs[0], cols)

    def group(base, last):
        for u in range(unroll):
            for cols in col_blocks:
                if not (last and u == unroll - 1):
                    scores(base + u + 1, bufs[(u + 1) % 2], cols)
                consume(base + u, bufs[u % 2], cols)

    def body(i, carry):
        group(i * unroll, False)
        return carry

    lax.fori_loop(0, nk // unroll - 1, body, 0)
    group(nk - unroll, True)
    acc = acc_ref[...]
    o_ref[0] = (acc[:dv] / acc[dv:dv + 1]).T.astype(BF16)


def flash_attn(qt, k, vt, dv, tq, tk):
    batch, n_heads, dk, seq_len = qt.shape
    dve = vt.shape[1] // n_heads
    nk = seq_len // tk
    unroll = min(FLASH_UNROLL, nk)
    assert unroll % 2 == 0 and nk % unroll == 0
    return pl.pallas_call(
        functools.partial(_flash_kernel, tk=tk, nk=nk, dv=dv, unroll=unroll),
        grid=(batch, n_heads, seq_len // tq),
        in_specs=[pl.BlockSpec((1, 1, dk, tq), lambda b, h, i: (b, h, 0, i)),
                  pl.BlockSpec((1, 1, seq_len, dk), lambda b, h, i: (b, h, 0, 0)),
                  pl.BlockSpec((1, dve, seq_len), lambda b, h, i: (b, h, 0))],
        out_specs=pl.BlockSpec((1, tq, dv), lambda b, h, i: (b, i, h)),
        out_shape=jax.ShapeDtypeStruct((batch, seq_len, n_heads * dv), BF16),
        scratch_shapes=[pltpu.VMEM((1, tq), F32), pltpu.VMEM((dve, tq), F32),
                        pltpu.VMEM((tk, tq), F32), pltpu.VMEM((tk, tq), F32),
                        pltpu.VMEM((1, tq), F32), pltpu.VMEM((1, tq), F32)],
        compiler_params=_params(3),
        name="flash_attn",
    )(qt, k, vt)


def _rows_to_block(rows):
    n = rows[0].shape[1]
    rid = lax.broadcasted_iota(jnp.int32, (SUBLANES, n), 0)
    blk = jnp.broadcast_to(rows[-1], (SUBLANES, n))
    for k in range(SUBLANES - 2, -1, -1):
        blk = jnp.where(rid == k, rows[k], blk)
    return blk


def _ssm_kernel(u_ref, toep_ref, w1_ref, w2_ref, dec_ref, y_ref, s_ref, xf_ref, xb_ref, *, n_seq, n_chunk, n_state):
    u = u_ref[0]
    s_ref[...] = _dot(u, w1_ref[0])
    dec = dec_ref[0]
    a_re, a_im = dec[0:1], dec[1:2]
    fwd_lane = lax.broadcasted_iota(jnp.int32, (1, LANES), 1) < n_state
    n_blk = n_chunk // SUBLANES

    def step(i, carry):
        new = []
        for q in range(n_seq):
            x_re, x_im = carry[2 * q], carry[2 * q + 1]
            rf = pl.multiple_of(q * n_chunk + i * SUBLANES, SUBLANES)
            rb = pl.multiple_of(q * n_chunk + (n_blk - 1 - i) * SUBLANES, SUBLANES)
            sf = s_ref[pl.ds(rf, SUBLANES), :]
            sb = s_ref[pl.ds(rb, SUBLANES), :]
            rows_re, rows_im = [], []
            for k in range(SUBLANES):
                kb = SUBLANES - 1 - k
                rows_re.append(x_re)
                rows_im.append(x_im)
                s_re = jnp.where(fwd_lane, sf[k:k + 1, :LANES], sb[kb:kb + 1, :LANES])
                s_im = jnp.where(fwd_lane, sf[k:k + 1, LANES:], sb[kb:kb + 1, LANES:])
                x_re, x_im = a_re * x_re - a_im * x_im + s_re, a_re * x_im + a_im * x_re + s_im
            xf_ref[pl.ds(rf, SUBLANES), :LANES] = _rows_to_block(rows_re)
            xf_ref[pl.ds(rf, SUBLANES), LANES:] = _rows_to_block(rows_im)
            xb_ref[pl.ds(rb, SUBLANES), :LANES] = _rows_to_block(rows_re[::-1])
            xb_ref[pl.ds(rb, SUBLANES), LANES:] = _rows_to_block(rows_im[::-1])
            new += [x_re, x_im]
        return tuple(new)

    zero = jnp.zeros((1, LANES), F32)
    lax.fori_loop(0, n_blk, step, (zero,) * (2 * n_seq))
    y_ref[0] = (_dot(u, toep_ref[0]) + _dot_nt(xf_ref[...].astype(BF16), w2_ref[0, 0])
                + _dot_nt(xb_ref[...].astype(BF16), w2_ref[0, 1])).astype(y_ref.dtype)


def ssm(u_rows, toep, w1, w2, dec, n_seq, n_chunk):
    groups, rows, tp = u_rows.shape
    n_state = w1.shape[2] // 4
    assert 2 * n_state == LANES and n_chunk % SUBLANES == 0
    g3 = lambda g: (g, 0, 0)
    return pl.pallas_call(
        functools.partial(_ssm_kernel, n_seq=n_seq, n_chunk=n_chunk, n_state=n_state),
        grid=(groups,),
        in_specs=[pl.BlockSpec((1, rows, tp), g3), pl.BlockSpec((1, tp, tp), g3),
                  pl.BlockSpec((1, tp, 4 * n_state), g3), pl.BlockSpec((1, 2, tp, 4 * n_state), lambda g: (g, 0, 0, 0)),
                  pl.BlockSpec((1, 2, LANES), g3)],
        out_specs=pl.BlockSpec((1, rows, tp), g3),
        out_shape=jax.ShapeDtypeStruct((groups, rows, tp), BF16),
        scratch_shapes=[pltpu.VMEM((rows, 4 * n_state), F32)] * 3,
        compiler_params=_params(1),
        name="ssm",
    )(u_rows, toep, w1, w2, dec)


def ssm_operators(a_re, a_im, log_step, b_re, b_im, c_re, c_im, d_skip, chunk):
    _, groups, n_state = a_re.shape
    p = b_re.shape[-1]
    step = jnp.exp(log_step)[..., None]
    mag = jnp.exp(step * a_re)
    ab_re, ab_im = mag * jnp.cos(step * a_im), mag * jnp.sin(step * a_im)
    den = a_re * a_re + a_im * a_im
    nr, ni = ab_re - 1.0, ab_im
    f_re = (nr * a_re + ni * a_im) / den
    f_im = (ni * a_re - nr * a_im) / den
    bb_re = f_re[..., None] * b_re - f_im[..., None] * b_im
    bb_im = f_re[..., None] * b_im + f_im[..., None] * b_re
    j = jnp.arange(chunk + 1, dtype=F32)[:, None, None, None]
    pw_mag = jnp.exp(j * (step * a_re)[None])
    pw_re = pw_mag * jnp.cos(j * (step * a_im)[None])
    pw_im = pw_mag * jnp.sin(j * (step * a_im)[None])
    def fb(x_f, x_b):
        return jnp.concatenate([x_f, x_b], axis=-1).transpose(1, 0, 2)

    def re_im(f):
        return jnp.stack([f(pw_re), f(pw_im)], axis=1)

    zero = jnp.zeros((chunk - 1, groups, n_state), F32)
    pw_lag = re_im(lambda w: fb(jnp.concatenate([zero, w[:chunk, 0]]), jnp.concatenate([w[chunk - 1::-1, 1], zero])))
    pw_lag = jnp.pad(pw_lag, ((0, 0), (0, 0), (0, 1), (0, 0)))
    pw_in = re_im(lambda w: fb(w[chunk - 1::-1, 0], w[:chunk, 1]))
    pw_out = re_im(lambda w: fb(w[1:chunk + 1, 0], w[chunk:0:-1, 1]))
    dec = jnp.stack([fb(pw_re[chunk:, 0], pw_re[chunk:, 1])[:, 0], fb(pw_im[chunk:, 0], pw_im[chunk:, 1])[:, 0]],
                    axis=1)
    bt = jnp.stack([jnp.concatenate([bb_re[0], bb_re[1]], axis=1), jnp.concatenate([bb_im[0], bb_im[1]], axis=1)],
                   axis=1).transpose(0, 1, 3, 2)
    cc = jnp.stack([jnp.concatenate([c_re[0], c_re[1]], axis=-1), jnp.concatenate([c_im[0], c_im[1]], axis=-1)],
                   axis=1)
    eye = jnp.eye(p, LANES, dtype=F32)
    skip = d_skip.reshape(groups, p)[:, :, None] * eye
    n_rows = -(-(2 * chunk - 1) * p // LANES) * LANES
    sel = jnp.zeros((n_rows, LANES), F32).at[(chunk - 1) * p:chunk * p].set(eye)
    toep, w1, w2 = ssm_tables(bt, cc, skip, sel.astype(BF16), pw_lag, pw_in, pw_out, chunk)
    return toep, w1, w2, dec


def _ssm_tables_kernel(bt_ref, c_ref, skip_ref, sel_ref, pwl_ref, pwi_ref, pwo_ref, toep_ref, w1_ref, w2_ref, rt_ref,
                       *, chunk, n_state):
    p = bt_ref.shape[2]
    bt_re, bt_im = bt_ref[0, 0], bt_ref[0, 1]
    c_re, c_im = c_ref[0, 0], c_ref[0, 1]
    rt_ref[...] = jnp.zeros(rt_ref.shape, BF16)
    for li in range(2 * chunk - 1):
        a_r, a_i = pwl_ref[0, 0, li:li + 1, :], pwl_ref[0, 1, li:li + 1, :]
        rt_ref[li * p:(li + 1) * p, :LANES] = (c_re * a_r - c_im * a_i).astype(BF16)
        rt_ref[li * p:(li + 1) * p, LANES:] = (c_re * a_i + c_im * a_r).astype(BF16)
    lhs = jnp.concatenate([bt_re, -bt_im], axis=1).astype(BF16)
    by_lag = _dot_nt(lhs, rt_ref[...]) + _dot_nt(skip_ref[0].astype(BF16), sel_ref[...])
    width = by_lag.shape[1]
    for s in range(chunk):
        off = (chunk - 1 - s) * p
        win = by_lag if off == 0 else pltpu.roll(by_lag, width - off, 1)
        toep_ref[0, s * p:(s + 1) * p, :] = win[:, :chunk * p].astype(BF16)
    fwd = lax.broadcasted_iota(jnp.int32, (p, LANES), 1) < n_state
    for s in range(chunk):
        rows = slice(s * p, (s + 1) * p)
        a_r, a_i = pwi_ref[0, 0, s:s + 1, :], pwi_ref[0, 1, s:s + 1, :]
        w1_ref[0, rows, :LANES] = (bt_re * a_r - bt_im * a_i).astype(BF16)
        w1_ref[0, rows, LANES:] = (bt_re * a_i + bt_im * a_r).astype(BF16)
        a_r, a_i = pwo_ref[0, 0, s:s + 1, :], pwo_ref[0, 1, s:s + 1, :]
        o_re = c_re * a_r - c_im * a_i
        o_im = -(c_re * a_i + c_im * a_r)
        w2_ref[0, 0, rows, :LANES] = jnp.where(fwd, o_re, 0.0).astype(BF16)
        w2_ref[0, 0, rows, LANES:] = jnp.where(fwd, o_im, 0.0).astype(BF16)
        w2_ref[0, 1, rows, :LANES] = jnp.where(fwd, 0.0, o_re).astype(BF16)
        w2_ref[0, 1, rows, LANES:] = jnp.where(fwd, 0.0, o_im).astype(BF16)


def ssm_tables(bt, cc, skip, sel, pw_lag, pw_in, pw_out, chunk):
    groups, _, p, lanes = bt.shape
    n_state = lanes // 2
    tp = chunk * p
    g4 = lambda g: (g, 0, 0, 0)
    blk4 = lambda a: pl.BlockSpec((1,) + a.shape[1:], g4)
    return pl.pallas_call(
        functools.partial(_ssm_tables_kernel, chunk=chunk, n_state=n_state),
        grid=(groups,),
        in_specs=[blk4(bt), blk4(cc), pl.BlockSpec((1, p, LANES), lambda g: (g, 0, 0)), _resident(sel.shape),
                  blk4(pw_lag), blk4(pw_in), blk4(pw_out)],
        out_specs=[pl.BlockSpec((1, tp, tp), lambda g: (g, 0, 0)), pl.BlockSpec((1, tp, 2 * lanes), lambda g: (g, 0, 0)),
                   pl.BlockSpec((1, 2, tp, 2 * lanes), g4)],
        out_shape=[jax.ShapeDtypeStruct((groups, tp, tp), BF16), jax.ShapeDtypeStruct((groups, tp, 2 * lanes), BF16),
                   jax.ShapeDtypeStruct((groups, 2, tp, 2 * lanes), BF16)],
        scratch_shapes=[pltpu.VMEM(sel.shape[:1] + (2 * lanes,), BF16)],
        compiler_params=_params(1),
        name="ssm_tables",
    )(bt, cc, skip, sel, pw_lag, pw_in, pw_out)


def lane_select(p):
    per = LANES // p
    r = jnp.arange(per * LANES)
    c = jnp.arange(2 * LANES)
    a, b, x = r // LANES, (r % LANES) // p, r % p
    h, a2, x2 = c // LANES, (c % LANES) // p, c % p
    k = jnp.arange(per // 2)[:, None, None]
    sel = (b[None, :, None] == 2 * k + h[None, None, :]) & (a[:, None] == a2[None, :]) & (x[:, None] == x2[None, :])
    return sel.astype(BF16)


def _to_rows_kernel(u_ref, sel_ref, o_ref, *, chunk):
    gps, rows, _ = o_ref.shape
    per = sel_ref.shape[1] // LANES
    for cb in range(chunk // per):
        cat = jnp.concatenate([u_ref[pl.ds(cb * per + s, rows, stride=chunk), :] for s in range(per)], axis=1)
        cat = cat.astype(BF16)
        for k in range(gps // 2):
            two = _dot(cat, sel_ref[k])
            o_ref[2 * k, :, cb * LANES:(cb + 1) * LANES] = two[:, :LANES].astype(o_ref.dtype)
            o_ref[2 * k + 1, :, cb * LANES:(cb + 1) * LANES] = two[:, LANES:].astype(o_ref.dtype)


def to_chunk_rows(u, sel, p, chunk, rows):
    t, width = u.shape
    gps = LANES // p
    return pl.pallas_call(
        functools.partial(_to_rows_kernel, chunk=chunk),
        grid=(width // LANES, t // (rows * chunk)),
        in_specs=[pl.BlockSpec((rows * chunk, LANES), lambda g, i: (i, g)), _resident(sel.shape)],
        out_specs=pl.BlockSpec((gps, rows, chunk * p), lambda g, i: (g, i, 0)),
        out_shape=jax.ShapeDtypeStruct((width // p, t // chunk, chunk * p), BF16),
        compiler_params=_params(2),
        name="to_chunk_rows",
    )(u, sel)


def _from_rows_kernel(y_ref, sel_ref, o_ref, *, chunk):
    gps, rows, _ = y_ref.shape
    per = sel_ref.shape[1] // LANES
    for cb in range(chunk // per):
        cat = jnp.concatenate([y_ref[g, :, cb * LANES:(cb + 1) * LANES] for g in range(gps)], axis=1)
        for k in range(per // 2):
            two = _dot(cat, sel_ref[k])
            o_ref[pl.ds(cb * per + 2 * k, rows, stride=chunk), :] = two[:, :LANES]
            o_ref[pl.ds(cb * per + 2 * k + 1, rows, stride=chunk), :] = two[:, LANES:]


def from_chunk_rows(y_rows, sel, p, chunk, rows):
    groups, n_rows, _ = y_rows.shape
    gps = LANES // p
    return pl.pallas_call(
        functools.partial(_from_rows_kernel, chunk=chunk),
        grid=(groups // gps, n_rows // rows),
        in_specs=[pl.BlockSpec((gps, rows, chunk * p), lambda g, i: (g, i, 0)), _resident(sel.shape)],
        out_specs=pl.BlockSpec((rows * chunk, LANES), lambda g, i: (i, g)),
        out_shape=jax.ShapeDtypeStruct((n_rows * chunk, groups * p), F32),
        compiler_params=_params(2),
        name="from_chunk_rows",
    )(y_rows, sel)


def _mix_kernel(xn_ref, o_ref, y_ref, wga_ref, wgs_ref, wo_ref, wz1_ref, wz2_ref, mix_ref, gy_ref):
    @pl.when(pl.program_id(1) == 0)
    def _():
        gy_ref[...] = jax.nn.gelu(y_ref[...].astype(F32)).astype(BF16)

    xn = xn_ref[...]
    gy = gy_ref[...]
    o_a = _dot(o_ref[...], wo_ref[...])
    o_s = _dot(gy, wz1_ref[...]) * jax.nn.sigmoid(_dot(gy, wz2_ref[...]))
    g_a = jax.nn.sigmoid(_dot(xn, wga_ref[...]))
    g_s = jax.nn.sigmoid(_dot(xn, wgs_ref[...]))
    mix_ref[...] = (g_a * o_a + g_s * o_s).astype(BF16)


def mix_merge(xn, o, y, wga, wgs, wo, wz1, wz2, tm, tn):
    t, d = xn.shape
    ho, sw = o.shape[1], y.shape[1]
    row = lambda i, j: (i, 0)
    col = lambda i, j: (0, j)
    return pl.pallas_call(
        _mix_kernel,
        grid=(t // tm, d // tn),
        in_specs=[pl.BlockSpec((tm, d), row), pl.BlockSpec((tm, ho), row), pl.BlockSpec((tm, sw), row),
                  pl.BlockSpec((d, tn), col), pl.BlockSpec((d, tn), col), pl.BlockSpec((ho, tn), col),
                  pl.BlockSpec((sw, tn), col), pl.BlockSpec((sw, tn), col)],
        out_specs=pl.BlockSpec((tm, tn), lambda i, j: (i, j)),
        out_shape=jax.ShapeDtypeStruct((t, d), BF16),
        scratch_shapes=[pltpu.VMEM((tm, sw), BF16)],
        compiler_params=_params(2, 1),
        name="mix_merge",
    )(xn, o, y, wga, wgs, wo, wz1, wz2)


def _resid_norm_kernel(x_ref, mix_ref, w_ref, g_ref, h_ref, n_ref):
    h = x_ref[...] + _dot(mix_ref[...], w_ref[...])
    h_ref[...] = h
    n_ref[...] = _rms(h, g_ref[...]).astype(BF16)


def resid_norm(x, mix, w, g, tm):
    t, d = x.shape
    row = lambda i: (i, 0)
    return pl.pallas_call(
        _resid_norm_kernel,
        grid=(t // tm,),
        in_specs=[pl.BlockSpec((tm, d), row), pl.BlockSpec((tm, d), row), _resident((d, d)), _resident((1, d))],
        out_specs=[pl.BlockSpec((tm, d), row), pl.BlockSpec((tm, d), row)],
        out_shape=[jax.ShapeDtypeStruct((t, d), F32), jax.ShapeDtypeStruct((t, d), BF16)],
        compiler_params=_params(1),
        name="resid_norm",
    )(x, mix, w, g.reshape(1, d))


def _mlp_kernel(n_ref, h_ref, w1_ref, w2_ref, o_ref, acc_ref):
    j = pl.program_id(1)

    @pl.when(j == 0)
    def _():
        acc_ref[...] = h_ref[...]

    a = jnp.maximum(_dot(n_ref[...], w1_ref[...]), 0.0)
    acc_ref[...] += _dot((a * a).astype(BF16), w2_ref[...])

    @pl.when(j == pl.num_programs(1) - 1)
    def _():
        o_ref[...] = acc_ref[...]


def mlp(n, h, w1, w2, tm, tf):
    t, d = h.shape
    ff = w1.shape[1]
    row = lambda i, j: (i, 0)
    return pl.pallas_call(
        _mlp_kernel,
        grid=(t // tm, ff // tf),
        in_specs=[pl.BlockSpec((tm, d), row), pl.BlockSpec((tm, d), row),
                  pl.BlockSpec((d, tf), lambda i, j: (0, j)), pl.BlockSpec((tf, d), lambda i, j: (j, 0))],
        out_specs=pl.BlockSpec((tm, d), row),
        out_shape=jax.ShapeDtypeStruct((t, d), F32),
        scratch_shapes=[pltpu.VMEM((tm, d), F32)],
        compiler_params=_params(2, 1),
        name="mlp",
    )(n, h, w1, w2)


def _ple_kernel(h_ref, p_ref, wg_ref, wp_ref, gp_ref, gf_ref, o_ref, *, final):
    h = h_ref[...]
    gate = jax.nn.sigmoid(_dot(_rms(h, gp_ref[...]).astype(BF16), wg_ref[...]))
    h = h + gate * _dot(p_ref[...].astype(BF16), wp_ref[...])
    o_ref[...] = _rms(h, gf_ref[...]) if final else h


def ple(h, p, wg, wp, gp, gf, tm, final):
    t, d = h.shape
    pd = p.shape[1]
    row = lambda i: (i, 0)
    return pl.pallas_call(
        functools.partial(_ple_kernel, final=final),
        grid=(t // tm,),
        in_specs=[pl.BlockSpec((tm, d), row), pl.BlockSpec((tm, pd), row), _resident((d, d)), _resident((pd, d)),
                  _resident((1, d)), _resident((1, d))],
        out_specs=pl.BlockSpec((tm, d), row),
        out_shape=jax.ShapeDtypeStruct((t, d), F32),
        compiler_params=_params(1),
        name="ple",
    )(h, p, wg, wp, gp.reshape(1, d), gf.reshape(1, d))


def _rope_tables(length, rope_dim):
    pos = jnp.arange(length, dtype=F32)
    inv = ROPE_THETA ** (-jnp.arange(0, rope_dim, 2, dtype=F32) / rope_dim)
    ang = pos[:, None] * inv[None, :]
    return jnp.cos(ang), jnp.sin(ang)


def _rope_lanes(cos, sin):
    length, half = cos.shape
    z = jnp.zeros((length, LANES // 2 - half), F32)
    return (jnp.concatenate([cos, z, cos, z], axis=1), jnp.concatenate([-sin, z, sin, z], axis=1))


def _spread_rope(w, half):
    z = jnp.zeros(w.shape[:-1] + (LANES // 2 - half,), w.dtype)
    return jnp.concatenate([w[..., :half], z, w[..., half:], z], axis=-1)


def _tile(n, pref):
    t = min(n, pref)
    while n % t:
        t //= 2
    return t


def _layer_weights(w_in, norm_q, w_uq, norm_kv, w_uk, w_uv, w_o_attn, ssm, w_glu, d_model):
    cq, ckv = norm_q.shape[0], norm_kv.shape[0]
    n_heads, dn = w_uk.shape[1], w_uk.shape[2]
    dv = w_uv.shape[2]
    sw = ssm[7].shape[0]
    rope = w_in.shape[1] - (cq + ckv + sw + 2 * d_model)
    half = rope // 2
    o_q, o_kv, o_kr, o_u, o_ga, o_gs = 0, cq, cq + ckv, cq + ckv + rope, cq + ckv + rope + sw, cq + ckv + rope + sw + d_model
    scale = (dn + rope) ** -0.5 * math.log2(math.e)
    wuq = w_uq.reshape(cq, n_heads, dn + rope) * scale
    wuq_t = jnp.concatenate([wuq[..., :dn], _spread_rope(wuq[..., dn:], half)], axis=-1).transpose(1, 2, 0)
    return dict(
        rope=rope, n_heads=n_heads, dn=dn, dv=dv,
        wq=w_in[:, o_q:o_kv].astype(BF16), wkv=w_in[:, o_kv:o_kr].astype(BF16),
        wkr=_spread_rope(w_in[:, o_kr:o_u], half).astype(BF16),
        wu=w_in[:, o_u:o_ga].astype(BF16), wga=w_in[:, o_ga:o_gs].astype(BF16), wgs=w_in[:, o_gs:].astype(BF16),
        wuq_t=wuq_t.astype(BF16),
        wuk=w_uk.reshape(ckv, n_heads * dn).astype(BF16),
        wuv_t=w_uv.reshape(ckv, n_heads * dv).T.astype(BF16),
        wo=w_o_attn.astype(BF16),
        wz1=w_glu[:, :d_model].astype(BF16), wz2=w_glu[:, d_model:].astype(BF16),
        ssm_ops=ssm_operators(*ssm, SSM_CHUNK),
        lane_sel=lane_select(ssm[3].shape[-1]),
    )


def _encoder_layer(h, p, lw, norm_mix, norm_q, norm_kv, w_out, norm_mlp, w_mlp_in, w_mlp_out,
                   norm_ple, w_ple_gate, w_ple, norm_out, final, batch, seq_len):
    t, d = h.shape
    tm = _tile(seq_len, 512)
    cos, sin = _rope_tables(seq_len, lw['rope'])
    cos_l, sin_l = _rope_lanes(cos, sin)
    xn, qn, ckv, kr, u = in_proj(h, norm_mix, lw['wq'], lw['wkv'], lw['wkr'], lw['wu'], norm_q, norm_kv,
                                 cos_l, sin_l, seq_len, tm)
    qt, k, vt = attn_prep(qn, ckv, kr, lw['wuq_t'], lw['wuk'], lw['wuv_t'], cos_l.T, sin_l.T, batch, seq_len, tm)
    o = flash_attn(qt, k, vt, lw['dv'], _tile(seq_len, FLASH_TQ), _tile(seq_len, FLASH_TK)).reshape(t, -1)
    toep, w1, w2, dec = lw['ssm_ops']
    groups = toep.shape[0]
    gp = lw['wu'].shape[1] // groups
    n_chunk = seq_len // SSM_CHUNK
    rows_tile = _tile(batch * n_chunk, 64)
    u_rows = to_chunk_rows(u, lw['lane_sel'], gp, SSM_CHUNK, rows_tile)
    y_rows = ssm(u_rows, toep, w1, w2, dec, batch, n_chunk)
    y = from_chunk_rows(y_rows, lw['lane_sel'], gp, SSM_CHUNK, rows_tile)
    mix = mix_merge(xn, o, y, lw['wga'], lw['wgs'], lw['wo'], lw['wz1'], lw['wz2'], tm, _tile(d, 512))
    h1, n1 = resid_norm(h, mix, w_out.astype(BF16), norm_mlp, tm)
    h2 = mlp(n1, h1, w_mlp_in.astype(BF16), w_mlp_out.astype(BF16), tm, _tile(w_mlp_in.shape[1], 1024))
    return ple(h2, p, w_ple_gate.astype(BF16), w_ple.astype(BF16), norm_ple, norm_out, tm, final)


def kernel(x_prompt, x_sample, p_prompt, p_sample, norm_mix, w_in, norm_q, w_uq, norm_kv, w_uk, w_uv, w_o_attn, ssm_a_re, ssm_a_im, ssm_log_step, ssm_b_re, ssm_b_im, ssm_c_re, ssm_c_im, ssm_d, w_glu, w_out, norm_mlp, w_mlp_in, w_mlp_out, norm_ple, w_ple_gate, w_ple, norm_final):
    depth = w_in.shape[0]
    d_model = x_prompt.shape[-1]
    layers = []
    for i in range(depth):
        ssm_w = (ssm_a_re[i], ssm_a_im[i], ssm_log_step[i], ssm_b_re[i], ssm_b_im[i], ssm_c_re[i], ssm_c_im[i], ssm_d[i])
        layers.append(_layer_weights(w_in[i], norm_q[i], w_uq[i], norm_kv[i], w_uk[i], w_uv[i], w_o_attn[i],
                                     ssm_w, w_glu[i], d_model))
    outs = []
    for x, p in ((x_prompt, p_prompt), (x_sample, p_sample)):
        batch, seq_len, _ = x.shape
        h = x.reshape(batch * seq_len, d_model)
        for i in range(depth):
            final = i == depth - 1
            h = _encoder_layer(h, p[i].reshape(batch * seq_len, -1), layers[i], norm_mix[i], norm_q[i], norm_kv[i],
                               w_out[i], norm_mlp[i], w_mlp_in[i], w_mlp_out[i], norm_ple[i], w_ple_gate[i], w_ple[i],
                               norm_final if final else norm_ple[i], final, batch, seq_len)
        outs.append(h.reshape(batch, seq_len, d_model))
    return tuple(outs)
```

```python
import functools
import math

import jax
import jax.numpy as jnp
from jax import lax
from jax.experimental import pallas as pl
from jax.experimental.pallas import tpu as pltpu

NORM_EPS = 1e-6
ROPE_THETA = 10000.0
LANES = 128
SUBLANES = 8
BF16_ROWS = 16
FLASH_TQ = 2048
FLASH_TK = 512
FLASH_UNROLL = 4
MXU_DIM = 256
VMEM_LIMIT = 56 * 1024 * 1024
SSM_CHUNK = 32
BF16 = jnp.bfloat16
F32 = jnp.float32


def _params(n_axes, n_parallel=None):
    n_parallel = n_axes if n_parallel is None else n_parallel
    sem = ("parallel",) * n_parallel + ("arbitrary",) * (n_axes - n_parallel)
    return pltpu.CompilerParams(dimension_semantics=sem, vmem_limit_bytes=VMEM_LIMIT)


def _resident(shape):
    nd = len(shape)
    return pl.BlockSpec(shape, lambda *_: (0,) * nd, pipeline_mode=pl.Buffered(1))


def _rms(x, g):
    r = lax.rsqrt(jnp.mean(x * x, axis=-1, keepdims=True) + NORM_EPS)
    return (x * r) * g


def _dot(a, b):
    return jnp.dot(a, b, preferred_element_type=F32)


def _dot_nt(a, b):
    return lax.dot_general(a, b, (((1,), (1,)), ((), ())), preferred_element_type=F32)


def _in_proj_kernel(x_ref, g_ref, wq_ref, wkv_ref, wkr_ref, wu_ref, gq_ref, gkv_ref, cos_ref, sin_ref,
                    xn_ref, qn_ref, ckv_ref, kr_ref, u_ref):
    xn = _rms(x_ref[...], g_ref[...]).astype(BF16)
    xn_ref[...] = xn
    qn_ref[...] = _rms(_dot(xn, wq_ref[...]), gq_ref[...]).astype(BF16)
    ckv_ref[...] = _rms(_dot(xn, wkv_ref[...]), gkv_ref[...]).astype(BF16)
    kr = _dot(xn, wkr_ref[...])
    kr_ref[...] = (kr * cos_ref[...] + pltpu.roll(kr, LANES // 2, 1) * sin_ref[...]).astype(BF16)
    u_ref[...] = _dot(xn, wu_ref[...])


def in_proj(x, g, wq, wkv, wkr, wu, gq, gkv, cos_l, sin_l, seq_len, tm):
    t, d = x.shape
    cq, ckv, sw = wq.shape[1], wkv.shape[1], wu.shape[1]
    npos = seq_len // tm
    row = lambda i: (i, 0)
    pos = lambda i: (i % npos, 0)
    widths = (d, cq, ckv, LANES, sw)
    return pl.pallas_call(
        _in_proj_kernel,
        grid=(t // tm,),
        in_specs=[pl.BlockSpec((tm, d), row), _resident((1, d)),
                  _resident((d, cq)), _resident((d, ckv)), _resident((d, LANES)), _resident((d, sw)),
                  _resident((1, cq)), _resident((1, ckv)),
                  pl.BlockSpec((tm, LANES), pos), pl.BlockSpec((tm, LANES), pos)],
        out_specs=[pl.BlockSpec((tm, w), row) for w in widths],
        out_shape=[jax.ShapeDtypeStruct((t, w), BF16) for w in widths[:-1]] + [jax.ShapeDtypeStruct((t, sw), F32)],
        compiler_params=_params(1),
        name="in_proj",
    )(x, g.reshape(1, d), wq, wkv, wkr, wu, gq.reshape(1, cq), gkv.reshape(1, ckv), cos_l, sin_l)


def _attn_prep_kernel(qn_ref, ckv_ref, kr_ref, wuq_ref, wuk_ref, wuv_ref, cos_ref, sin_ref,
                      qt_ref, k_ref, vt_ref, *, n_heads, dn, dv):
    qn = qn_ref[...]
    ckv = ckv_ref[...]
    cos_t = cos_ref[...]
    sin_t = sin_ref[...]
    half = LANES // 2
    for h in range(n_heads):
        q = _dot_nt(wuq_ref[h], qn)
        qr = q[dn:]
        qr_sw = jnp.concatenate([qr[half:], qr[:half]], axis=0)
        qt_ref[0, h, :dn, :] = q[:dn].astype(BF16)
        qt_ref[0, h, dn:, :] = (qr * cos_t + qr_sw * sin_t).astype(BF16)
    k_all = _dot(ckv, wuk_ref[...])
    kr = kr_ref[...]
    for h in range(n_heads):
        k_ref[0, h, :, :dn] = k_all[:, h * dn:(h + 1) * dn].astype(BF16)
        k_ref[0, h, :, dn:] = kr
    vt = _dot_nt(wuv_ref[...], ckv).astype(BF16)
    tm = vt.shape[1]
    ones_rows = (lax.broadcasted_iota(jnp.int32, (BF16_ROWS, tm), 0) == 0).astype(BF16)
    dve = dv + BF16_ROWS
    for h in range(n_heads):
        vt_ref[0, h * dve:h * dve + dv, :] = vt[h * dv:(h + 1) * dv]
        vt_ref[0, h * dve + dv:(h + 1) * dve, :] = ones_rows


def attn_prep(qn, ckv, kr, wuq_t, wuk, wuv_t, cos_t, sin_t, batch, seq_len, tm):
    n_heads, dk, cq = wuq_t.shape
    dn = dk - LANES
    dv = wuv_t.shape[0] // n_heads
    hdv = n_heads * (dv + BF16_ROWS)
    ckv_dim = ckv.shape[1]
    npos = seq_len // tm
    row = lambda b, i: (b * npos + i, 0)
    col = lambda b, i: (0, i)
    return pl.pallas_call(
        functools.partial(_attn_prep_kernel, n_heads=n_heads, dn=dn, dv=dv),
        grid=(batch, npos),
        in_specs=[pl.BlockSpec((tm, cq), row), pl.BlockSpec((tm, ckv_dim), row), pl.BlockSpec((tm, LANES), row),
                  _resident((n_heads, dk, cq)), _resident((ckv_dim, n_heads * dn)), _resident(wuv_t.shape),
                  pl.BlockSpec((LANES, tm), col), pl.BlockSpec((LANES, tm), col)],
        out_specs=[pl.BlockSpec((1, n_heads, dk, tm), lambda b, i: (b, 0, 0, i)),
                   pl.BlockSpec((1, n_heads, tm, dk), lambda b, i: (b, 0, i, 0)),
                   pl.BlockSpec((1, hdv, tm), lambda b, i: (b, 0, i))],
        out_shape=[jax.ShapeDtypeStruct((batch, n_heads, dk, seq_len), BF16),
                   jax.ShapeDtypeStruct((batch, n_heads, seq_len, dk), BF16),
                   jax.ShapeDtypeStruct((batch, hdv, seq_len), BF16)],
        compiler_params=_params(2),
        name="attn_prep",
    )(qn, ckv, kr, wuq_t, wuk, wuv_t, cos_t, sin_t)


def _flash_kernel(qt_ref, k_ref, vt_ref, o_ref, m_ref, acc_ref, s0_ref, s1_ref, c0_ref, c1_ref,
                  *, tk, nk, dv, unroll):
    qt = qt_ref[0, 0]
    tq = qt.shape[1]
    m_ref[...] = jnp.full(m_ref.shape, -jnp.inf, F32)
    acc_ref[...] = jnp.zeros(acc_ref.shape, F32)

    col_blocks = [slice(c, c + MXU_DIM) for c in range(0, tq, MXU_DIM)]

    def scores(j, buf, cols):
        s_ref, c_ref = buf
        off = pl.multiple_of(j * tk, tk)
        s = _dot(k_ref[0, 0, pl.ds(off, tk), :], qt[:, cols])
        s_ref[:, cols] = s
        c_ref[:, cols] = jnp.max(s, axis=0, keepdims=True)

    def consume(j, buf, cols):
        s_ref, c_ref = buf
        off = pl.multiple_of(j * tk, tk)
        m_prev = m_ref[:, cols]
        m_new = jnp.maximum(m_prev, c_ref[:, cols])
        alpha = jnp.exp2(m_prev - m_new)
        p = jnp.exp2(s_ref[:, cols] - m_new).astype(BF16)
        acc_ref[:, cols] = alpha * acc_ref[:, cols] + _dot(vt_ref[0, :, pl.ds(off, tk)], p)
        m_ref[:, cols] = m_new

    bufs = ((s0_ref, c0_ref), (s1_ref, c1_ref))
    for cols in col_blocks:
        scores(0, bufs[0], cols)

    def group(base, last):
        n_blk = len(col_blocks)
        for u in range(unroll):
            for b in range(n_blk):
                if not (last and u == unroll - 1):
                    scores(base + u + 1, bufs[(u + 1) % 2], col_blocks[b])
                consume(base + u, bufs[u % 2], col_blocks[(b + 1) % n_blk])

    def body(i, carry):
        group(i * unroll, False)
        return carry

    lax.fori_loop(0, nk // unroll - 1, body, 0)
    group(nk - unroll, True)
    acc = acc_ref[...]
    o_ref[0] = (acc[:dv] / acc[dv:dv + 1]).T.astype(BF16)


def flash_attn(qt, k, vt, dv, tq, tk):
    batch, n_heads, dk, seq_len = qt.shape
    dve = vt.shape[1] // n_heads
    nk = seq_len // tk
    unroll = min(FLASH_UNROLL, nk)
    assert unroll % 2 == 0 and nk % unroll == 0
    return pl.pallas_call(
        functools.partial(_flash_kernel, tk=tk, nk=nk, dv=dv, unroll=unroll),
        grid=(batch, n_heads, seq_len // tq),
        in_specs=[pl.BlockSpec((1, 1, dk, tq), lambda b, h, i: (b, h, 0, i)),
                  pl.BlockSpec((1, 1, seq_len, dk), lambda b, h, i: (b, h, 0, 0)),
                  pl.BlockSpec((1, dve, seq_len), lambda b, h, i: (b, h, 0))],
        out_specs=pl.BlockSpec((1, tq, dv), lambda b, h, i: (b, i, h)),
        out_shape=jax.ShapeDtypeStruct((batch, seq_len, n_heads * dv), BF16),
        scratch_shapes=[pltpu.VMEM((1, tq), F32), pltpu.VMEM((dve, tq), F32),
                        pltpu.VMEM((tk, tq), F32), pltpu.VMEM((tk, tq), F32),
                        pltpu.VMEM((1, tq), F32), pltpu.VMEM((1, tq), F32)],
        compiler_params=_params(3),
        name="flash_attn",
    )(qt, k, vt)


def _rows_to_block(rows):
    n = rows[0].shape[1]
    rid = lax.broadcasted_iota(jnp.int32, (SUBLANES, n), 0)
    blk = jnp.broadcast_to(rows[-1], (SUBLANES, n))
    for k in range(SUBLANES - 2, -1, -1):
        blk = jnp.where(rid == k, rows[k], blk)
    return blk


def _ssm_kernel(u_ref, toep_ref, w1_ref, w2_ref, dec_ref, y_ref, s_ref, xf_ref, xb_ref, *, n_seq, n_chunk, n_state):
    u = u_ref[0]
    s_ref[...] = _dot(u, w1_ref[0])
    dec = dec_ref[0]
    a_re, a_im = dec[0:1], dec[1:2]
    fwd_lane = lax.broadcasted_iota(jnp.int32, (1, LANES), 1) < n_state
    n_blk = n_chunk // SUBLANES

    def step(i, carry):
        new = []
        for q in range(n_seq):
            x_re, x_im = carry[2 * q], carry[2 * q + 1]
            rf = pl.multiple_of(q * n_chunk + i * SUBLANES, SUBLANES)
            rb = pl.multiple_of(q * n_chunk + (n_blk - 1 - i) * SUBLANES, SUBLANES)
            sf = s_ref[pl.ds(rf, SUBLANES), :]
            sb = s_ref[pl.ds(rb, SUBLANES), :]
            rows_re, rows_im = [], []
            for k in range(SUBLANES):
                kb = SUBLANES - 1 - k
                rows_re.append(x_re)
                rows_im.append(x_im)
                s_re = jnp.where(fwd_lane, sf[k:k + 1, :LANES], sb[kb:kb + 1, :LANES])
                s_im = jnp.where(fwd_lane, sf[k:k + 1, LANES:], sb[kb:kb + 1, LANES:])
                x_re, x_im = a_re * x_re - a_im * x_im + s_re, a_re * x_im + a_im * x_re + s_im
            xf_ref[pl.ds(rf, SUBLANES), :LANES] = _rows_to_block(rows_re)
            xf_ref[pl.ds(rf, SUBLANES), LANES:] = _rows_to_block(rows_im)
            xb_ref[pl.ds(rb, SUBLANES), :LANES] = _rows_to_block(rows_re[::-1])
            xb_ref[pl.ds(rb, SUBLANES), LANES:] = _rows_to_block(rows_im[::-1])
            new += [x_re, x_im]
        return tuple(new)

    zero = jnp.zeros((1, LANES), F32)
    lax.fori_loop(0, n_blk, step, (zero,) * (2 * n_seq))
    y_ref[0] = (_dot(u, toep_ref[0]) + _dot_nt(xf_ref[...].astype(BF16), w2_ref[0, 0])
                + _dot_nt(xb_ref[...].astype(BF16), w2_ref[0, 1])).astype(y_ref.dtype)


def ssm(u_rows, toep, w1, w2, dec, n_seq, n_chunk):
    groups, rows, tp = u_rows.shape
    n_state = w1.shape[2] // 4
    assert 2 * n_state == LANES and n_chunk % SUBLANES == 0
    g3 = lambda g: (g, 0, 0)
    return pl.pallas_call(
        functools.partial(_ssm_kernel, n_seq=n_seq, n_chunk=n_chunk, n_state=n_state),
        grid=(groups,),
        in_specs=[pl.BlockSpec((1, rows, tp), g3), pl.BlockSpec((1, tp, tp), g3),
                  pl.BlockSpec((1, tp, 4 * n_state), g3), pl.BlockSpec((1, 2, tp, 4 * n_state), lambda g: (g, 0, 0, 0)),
                  pl.BlockSpec((1, 2, LANES), g3)],
        out_specs=pl.BlockSpec((1, rows, tp), g3),
        out_shape=jax.ShapeDtypeStruct((groups, rows, tp), BF16),
        scratch_shapes=[pltpu.VMEM((rows, 4 * n_state), F32)] * 3,
        compiler_params=_params(1),
        name="ssm",
    )(u_rows, toep, w1, w2, dec)


def ssm_operators(a_re, a_im, log_step, b_re, b_im, c_re, c_im, d_skip, chunk):
    _, groups, n_state = a_re.shape
    p = b_re.shape[-1]
    step = jnp.exp(log_step)[..., None]
    mag = jnp.exp(step * a_re)
    ab_re, ab_im = mag * jnp.cos(step * a_im), mag * jnp.sin(step * a_im)
    den = a_re * a_re + a_im * a_im
    nr, ni = ab_re - 1.0, ab_im
    f_re = (nr * a_re + ni * a_im) / den
    f_im = (ni * a_re - nr * a_im) / den
    bb_re = f_re[..., None] * b_re - f_im[..., None] * b_im
    bb_im = f_re[..., None] * b_im + f_im[..., None] * b_re
    j = jnp.arange(chunk + 1, dtype=F32)[:, None, None, None]
    pw_mag = jnp.exp(j * (step * a_re)[None])
    pw_re = pw_mag * jnp.cos(j * (step * a_im)[None])
    pw_im = pw_mag * jnp.sin(j * (step * a_im)[None])
    def fb(x_f, x_b):
        return jnp.concatenate([x_f, x_b], axis=-1).transpose(1, 0, 2)

    def re_im(f):
        return jnp.stack([f(pw_re), f(pw_im)], axis=1)

    zero = jnp.zeros((chunk - 1, groups, n_state), F32)
    pw_lag = re_im(lambda w: fb(jnp.concatenate([zero, w[:chunk, 0]]), jnp.concatenate([w[chunk - 1::-1, 1], zero])))
    pw_lag = jnp.pad(pw_lag, ((0, 0), (0, 0), (0, 1), (0, 0)))
    pw_in = re_im(lambda w: fb(w[chunk - 1::-1, 0], w[:chunk, 1]))
    pw_out = re_im(lambda w: fb(w[1:chunk + 1, 0], w[chunk:0:-1, 1]))
    dec = jnp.stack([fb(pw_re[chunk:, 0], pw_re[chunk:, 1])[:, 0], fb(pw_im[chunk:, 0], pw_im[chunk:, 1])[:, 0]],
                    axis=1)
    bt = jnp.stack([jnp.concatenate([bb_re[0], bb_re[1]], axis=1), jnp.concatenate([bb_im[0], bb_im[1]], axis=1)],
                   axis=1).transpose(0, 1, 3, 2)
    cc = jnp.stack([jnp.concatenate([c_re[0], c_re[1]], axis=-1), jnp.concatenate([c_im[0], c_im[1]], axis=-1)],
                   axis=1)
    eye = jnp.eye(p, LANES, dtype=F32)
    skip = d_skip.reshape(groups, p)[:, :, None] * eye
    n_rows = -(-(2 * chunk - 1) * p // LANES) * LANES
    sel = jnp.zeros((n_rows, LANES), F32).at[(chunk - 1) * p:chunk * p].set(eye)
    toep, w1, w2 = ssm_tables(bt, cc, skip, sel.astype(BF16), pw_lag, pw_in, pw_out, chunk)
    return toep, w1, w2, dec


def _ssm_tables_kernel(bt_ref, c_ref, skip_ref, sel_ref, pwl_ref, pwi_ref, pwo_ref, toep_ref, w1_ref, w2_ref, rt_ref,
                       *, chunk, n_state):
    p = bt_ref.shape[2]
    bt_re, bt_im = bt_ref[0, 0], bt_ref[0, 1]
    c_re, c_im = c_ref[0, 0], c_ref[0, 1]
    rt_ref[...] = jnp.zeros(rt_ref.shape, BF16)
    for li in range(2 * chunk - 1):
        a_r, a_i = pwl_ref[0, 0, li:li + 1, :], pwl_ref[0, 1, li:li + 1, :]
        rt_ref[li * p:(li + 1) * p, :LANES] = (c_re * a_r - c_im * a_i).astype(BF16)
        rt_ref[li * p:(li + 1) * p, LANES:] = (c_re * a_i + c_im * a_r).astype(BF16)
    lhs = jnp.concatenate([bt_re, -bt_im], axis=1).astype(BF16)
    by_lag = _dot_nt(lhs, rt_ref[...]) + _dot_nt(skip_ref[0].astype(BF16), sel_ref[...])
    width = by_lag.shape[1]
    for s in range(chunk):
        off = (chunk - 1 - s) * p
        win = by_lag if off == 0 else pltpu.roll(by_lag, width - off, 1)
        toep_ref[0, s * p:(s + 1) * p, :] = win[:, :chunk * p].astype(BF16)
    fwd = lax.broadcasted_iota(jnp.int32, (p, LANES), 1) < n_state
    for s in range(chunk):
        rows = slice(s * p, (s + 1) * p)
        a_r, a_i = pwi_ref[0, 0, s:s + 1, :], pwi_ref[0, 1, s:s + 1, :]
        w1_ref[0, rows, :LANES] = (bt_re * a_r - bt_im * a_i).astype(BF16)
        w1_ref[0, rows, LANES:] = (bt_re * a_i + bt_im * a_r).astype(BF16)
        a_r, a_i = pwo_ref[0, 0, s:s + 1, :], pwo_ref[0, 1, s:s + 1, :]
        o_re = c_re * a_r - c_im * a_i
        o_im = -(c_re * a_i + c_im * a_r)
        w2_ref[0, 0, rows, :LANES] = jnp.where(fwd, o_re, 0.0).astype(BF16)
        w2_ref[0, 0, rows, LANES:] = jnp.where(fwd, o_im, 0.0).astype(BF16)
        w2_ref[0, 1, rows, :LANES] = jnp.where(fwd, 0.0, o_re).astype(BF16)
        w2_ref[0, 1, rows, LANES:] = jnp.where(fwd, 0.0, o_im).astype(BF16)


def ssm_tables(bt, cc, skip, sel, pw_lag, pw_in, pw_out, chunk):
    groups, _, p, lanes = bt.shape
    n_state = lanes // 2
    tp = chunk * p
    g4 = lambda g: (g, 0, 0, 0)
    blk4 = lambda a: pl.BlockSpec((1,) + a.shape[1:], g4)
    return pl.pallas_call(
        functools.partial(_ssm_tables_kernel, chunk=chunk, n_state=n_state),
        grid=(groups,),
        in_specs=[blk4(bt), blk4(cc), pl.BlockSpec((1, p, LANES), lambda g: (g, 0, 0)), _resident(sel.shape),
                  blk4(pw_lag), blk4(pw_in), blk4(pw_out)],
        out_specs=[pl.BlockSpec((1, tp, tp), lambda g: (g, 0, 0)), pl.BlockSpec((1, tp, 2 * lanes), lambda g: (g, 0, 0)),
                   pl.BlockSpec((1, 2, tp, 2 * lanes), g4)],
        out_shape=[jax.ShapeDtypeStruct((groups, tp, tp), BF16), jax.ShapeDtypeStruct((groups, tp, 2 * lanes), BF16),
                   jax.ShapeDtypeStruct((groups, 2, tp, 2 * lanes), BF16)],
        scratch_shapes=[pltpu.VMEM(sel.shape[:1] + (2 * lanes,), BF16)],
        compiler_params=_params(1),
        name="ssm_tables",
    )(bt, cc, skip, sel, pw_lag, pw_in, pw_out)


def lane_select(p):
    per = LANES // p
    r = jnp.arange(per * LANES)
    c = jnp.arange(2 * LANES)
    a, b, x = r // LANES, (r % LANES) // p, r % p
    h, a2, x2 = c // LANES, (c % LANES) // p, c % p
    k = jnp.arange(per // 2)[:, None, None]
    sel = (b[None, :, None] == 2 * k + h[None, None, :]) & (a[:, None] == a2[None, :]) & (x[:, None] == x2[None, :])
    return sel.astype(BF16)


def _to_rows_kernel(u_ref, sel_ref, o_ref, *, chunk):
    gps, rows, _ = o_ref.shape
    per = sel_ref.shape[1] // LANES
    for cb in range(chunk // per):
        cat = jnp.concatenate([u_ref[pl.ds(cb * per + s, rows, stride=chunk), :] for s in range(per)], axis=1)
        cat = cat.astype(BF16)
        for k in range(gps // 2):
            two = _dot(cat, sel_ref[k])
            o_ref[2 * k, :, cb * LANES:(cb + 1) * LANES] = two[:, :LANES].astype(o_ref.dtype)
            o_ref[2 * k + 1, :, cb * LANES:(cb + 1) * LANES] = two[:, LANES:].astype(o_ref.dtype)


def to_chunk_rows(u, sel, p, chunk, rows):
    t, width = u.shape
    gps = LANES // p
    return pl.pallas_call(
        functools.partial(_to_rows_kernel, chunk=chunk),
        grid=(width // LANES, t // (rows * chunk)),
        in_specs=[pl.BlockSpec((rows * chunk, LANES), lambda g, i: (i, g)), _resident(sel.shape)],
        out_specs=pl.BlockSpec((gps, rows, chunk * p), lambda g, i: (g, i, 0)),
        out_shape=jax.ShapeDtypeStruct((width // p, t // chunk, chunk * p), BF16),
        compiler_params=_params(2),
        name="to_chunk_rows",
    )(u, sel)


def _from_rows_kernel(y_ref, sel_ref, o_ref, *, chunk):
    gps, rows, _ = y_ref.shape
    per = sel_ref.shape[1] // LANES
    for cb in range(chunk // per):
        cat = jnp.concatenate([y_ref[g, :, cb * LANES:(cb + 1) * LANES] for g in range(gps)], axis=1)
        for k in range(per // 2):
            two = _dot(cat, sel_ref[k])
            o_ref[pl.ds(cb * per + 2 * k, rows, stride=chunk), :] = two[:, :LANES]
            o_ref[pl.ds(cb * per + 2 * k + 1, rows, stride=chunk), :] = two[:, LANES:]


def from_chunk_rows(y_rows, sel, p, chunk, rows):
    groups, n_rows, _ = y_rows.shape
    gps = LANES // p
    return pl.pallas_call(
        functools.partial(_from_rows_kernel, chunk=chunk),
        grid=(groups // gps, n_rows // rows),
        in_specs=[pl.BlockSpec((gps, rows, chunk * p), lambda g, i: (g, i, 0)), _resident(sel.shape)],
        out_specs=pl.BlockSpec((rows * chunk, LANES), lambda g, i: (i, g)),
        out_shape=jax.ShapeDtypeStruct((n_rows * chunk, groups * p), F32),
        compiler_params=_params(2),
        name="from_chunk_rows",
    )(y_rows, sel)


def _mix_kernel(xn_ref, o_ref, y_ref, wga_ref, wgs_ref, wo_ref, wz1_ref, wz2_ref, mix_ref, gy_ref):
    @pl.when(pl.program_id(1) == 0)
    def _():
        gy_ref[...] = jax.nn.gelu(y_ref[...].astype(F32)).astype(BF16)

    xn = xn_ref[...]
    gy = gy_ref[...]
    o_a = _dot(o_ref[...], wo_ref[...])
    o_s = _dot(gy, wz1_ref[...]) * jax.nn.sigmoid(_dot(gy, wz2_ref[...]))
    g_a = jax.nn.sigmoid(_dot(xn, wga_ref[...]))
    g_s = jax.nn.sigmoid(_dot(xn, wgs_ref[...]))
    mix_ref[...] = (g_a * o_a + g_s * o_s).astype(BF16)


def mix_merge(xn, o, y, wga, wgs, wo, wz1, wz2, tm, tn):
    t, d = xn.shape
    ho, sw = o.shape[1], y.shape[1]
    row = lambda i, j: (i, 0)
    col = lambda i, j: (0, j)
    return pl.pallas_call(
        _mix_kernel,
        grid=(t // tm, d // tn),
        in_specs=[pl.BlockSpec((tm, d), row), pl.BlockSpec((tm, ho), row), pl.BlockSpec((tm, sw), row),
                  pl.BlockSpec((d, tn), col), pl.BlockSpec((d, tn), col), pl.BlockSpec((ho, tn), col),
                  pl.BlockSpec((sw, tn), col), pl.BlockSpec((sw, tn), col)],
        out_specs=pl.BlockSpec((tm, tn), lambda i, j: (i, j)),
        out_shape=jax.ShapeDtypeStruct((t, d), BF16),
        scratch_shapes=[pltpu.VMEM((tm, sw), BF16)],
        compiler_params=_params(2, 1),
        name="mix_merge",
    )(xn, o, y, wga, wgs, wo, wz1, wz2)


def _resid_norm_kernel(x_ref, mix_ref, w_ref, g_ref, h_ref, n_ref):
    h = x_ref[...] + _dot(mix_ref[...], w_ref[...])
    h_ref[...] = h
    n_ref[...] = _rms(h, g_ref[...]).astype(BF16)


def resid_norm(x, mix, w, g, tm):
    t, d = x.shape
    row = lambda i: (i, 0)
    return pl.pallas_call(
        _resid_norm_kernel,
        grid=(t // tm,),
        in_specs=[pl.BlockSpec((tm, d), row), pl.BlockSpec((tm, d), row), _resident((d, d)), _resident((1, d))],
        out_specs=[pl.BlockSpec((tm, d), row), pl.BlockSpec((tm, d), row)],
        out_shape=[jax.ShapeDtypeStruct((t, d), F32), jax.ShapeDtypeStruct((t, d), BF16)],
        compiler_params=_params(1),
        name="resid_norm",
    )(x, mix, w, g.reshape(1, d))


def _mlp_kernel(n_ref, h_ref, w1_ref, w2_ref, o_ref, acc_ref):
    j = pl.program_id(1)

    @pl.when(j == 0)
    def _():
        acc_ref[...] = h_ref[...]

    a = jnp.maximum(_dot(n_ref[...], w1_ref[...]), 0.0)
    acc_ref[...] += _dot((a * a).astype(BF16), w2_ref[...])

    @pl.when(j == pl.num_programs(1) - 1)
    def _():
        o_ref[...] = acc_ref[...]


def mlp(n, h, w1, w2, tm, tf):
    t, d = h.shape
    ff = w1.shape[1]
    row = lambda i, j: (i, 0)
    return pl.pallas_call(
        _mlp_kernel,
        grid=(t // tm, ff // tf),
        in_specs=[pl.BlockSpec((tm, d), row), pl.BlockSpec((tm, d), row),
                  pl.BlockSpec((d, tf), lambda i, j: (0, j)), pl.BlockSpec((tf, d), lambda i, j: (j, 0))],
        out_specs=pl.BlockSpec((tm, d), row),
        out_shape=jax.ShapeDtypeStruct((t, d), F32),
        scratch_shapes=[pltpu.VMEM((tm, d), F32)],
        compiler_params=_params(2, 1),
        name="mlp",
    )(n, h, w1, w2)


def _ple_kernel(h_ref, p_ref, wg_ref, wp_ref, gp_ref, gf_ref, o_ref, *, final):
    h = h_ref[...]
    gate = jax.nn.sigmoid(_dot(_rms(h, gp_ref[...]).astype(BF16), wg_ref[...]))
    h = h + gate * _dot(p_ref[...].astype(BF16), wp_ref[...])
    o_ref[...] = _rms(h, gf_ref[...]) if final else h


def ple(h, p, wg, wp, gp, gf, tm, final):
    t, d = h.shape
    pd = p.shape[1]
    row = lambda i: (i, 0)
    return pl.pallas_call(
        functools.partial(_ple_kernel, final=final),
        grid=(t // tm,),
        in_specs=[pl.BlockSpec((tm, d), row), pl.BlockSpec((tm, pd), row), _resident((d, d)), _resident((pd, d)),
                  _resident((1, d)), _resident((1, d))],
        out_specs=pl.BlockSpec((tm, d), row),
        out_shape=jax.ShapeDtypeStruct((t, d), F32),
        compiler_params=_params(1),
        name="ple",
    )(h, p, wg, wp, gp.reshape(1, d), gf.reshape(1, d))


def _rope_tables(length, rope_dim):
    pos = jnp.arange(length, dtype=F32)
    inv = ROPE_THETA ** (-jnp.arange(0, rope_dim, 2, dtype=F32) / rope_dim)
    ang = pos[:, None] * inv[None, :]
    return jnp.cos(ang), jnp.sin(ang)


def _rope_lanes(cos, sin):
    length, half = cos.shape
    z = jnp.zeros((length, LANES // 2 - half), F32)
    return (jnp.concatenate([cos, z, cos, z], axis=1), jnp.concatenate([-sin, z, sin, z], axis=1))


def _spread_rope(w, half):
    z = jnp.zeros(w.shape[:-1] + (LANES // 2 - half,), w.dtype)
    return jnp.concatenate([w[..., :half], z, w[..., half:], z], axis=-1)


def _tile(n, pref):
    t = min(n, pref)
    while n % t:
        t //= 2
    return t


def _layer_weights(w_in, norm_q, w_uq, norm_kv, w_uk, w_uv, w_o_attn, ssm, w_glu, d_model):
    cq, ckv = norm_q.shape[0], norm_kv.shape[0]
    n_heads, dn = w_uk.shape[1], w_uk.shape[2]
    dv = w_uv.shape[2]
    sw = ssm[7].shape[0]
    rope = w_in.shape[1] - (cq + ckv + sw + 2 * d_model)
    half = rope // 2
    o_q, o_kv, o_kr, o_u, o_ga, o_gs = 0, cq, cq + ckv, cq + ckv + rope, cq + ckv + rope + sw, cq + ckv + rope + sw + d_model
    scale = (dn + rope) ** -0.5 * math.log2(math.e)
    wuq = w_uq.reshape(cq, n_heads, dn + rope) * scale
    wuq_t = jnp.concatenate([wuq[..., :dn], _spread_rope(wuq[..., dn:], half)], axis=-1).transpose(1, 2, 0)
    return dict(
        rope=rope, n_heads=n_heads, dn=dn, dv=dv,
        wq=w_in[:, o_q:o_kv].astype(BF16), wkv=w_in[:, o_kv:o_kr].astype(BF16),
        wkr=_spread_rope(w_in[:, o_kr:o_u], half).astype(BF16),
        wu=w_in[:, o_u:o_ga].astype(BF16), wga=w_in[:, o_ga:o_gs].astype(BF16), wgs=w_in[:, o_gs:].astype(BF16),
        wuq_t=wuq_t.astype(BF16),
        wuk=w_uk.reshape(ckv, n_heads * dn).astype(BF16),
        wuv_t=w_uv.reshape(ckv, n_heads * dv).T.astype(BF16),
        wo=w_o_attn.astype(BF16),
        wz1=w_glu[:, :d_model].astype(BF16), wz2=w_glu[:, d_model:].astype(BF16),
        ssm_ops=ssm_operators(*ssm, SSM_CHUNK),
        lane_sel=lane_select(ssm[3].shape[-1]),
    )


def _encoder_layer(h, p, lw, norm_mix, norm_q, norm_kv, w_out, norm_mlp, w_mlp_in, w_mlp_out,
                   norm_ple, w_ple_gate, w_ple, norm_out, final, batch, seq_len):
    t, d = h.shape
    tm = _tile(seq_len, 512)
    cos, sin = _rope_tables(seq_len, lw['rope'])
    cos_l, sin_l = _rope_lanes(cos, sin)
    xn, qn, ckv, kr, u = in_proj(h, norm_mix, lw['wq'], lw['wkv'], lw['wkr'], lw['wu'], norm_q, norm_kv,
                                 cos_l, sin_l, seq_len, tm)
    qt, k, vt = attn_prep(qn, ckv, kr, lw['wuq_t'], lw['wuk'], lw['wuv_t'], cos_l.T, sin_l.T, batch, seq_len, tm)
    o = flash_attn(qt, k, vt, lw['dv'], _tile(seq_len, FLASH_TQ), _tile(seq_len, FLASH_TK)).reshape(t, -1)
    toep, w1, w2, dec = lw['ssm_ops']
    groups = toep.shape[0]
    gp = lw['wu'].shape[1] // groups
    n_chunk = seq_len // SSM_CHUNK
    rows_tile = _tile(batch * n_chunk, 64)
    u_rows = to_chunk_rows(u, lw['lane_sel'], gp, SSM_CHUNK, rows_tile)
    y_rows = ssm(u_rows, toep, w1, w2, dec, batch, n_chunk)
    y = from_chunk_rows(y_rows, lw['lane_sel'], gp, SSM_CHUNK, rows_tile)
    mix = mix_merge(xn, o, y, lw['wga'], lw['wgs'], lw['wo'], lw['wz1'], lw['wz2'], tm, _tile(d, 512))
    h1, n1 = resid_norm(h, mix, w_out.astype(BF16), norm_mlp, tm)
    h2 = mlp(n1, h1, w_mlp_in.astype(BF16), w_mlp_out.astype(BF16), tm, _tile(w_mlp_in.shape[1], 1024))
    return ple(h2, p, w_ple_gate.astype(BF16), w_ple.astype(BF16), norm_ple, norm_out, tm, final)


def kernel(x_prompt, x_sample, p_prompt, p_sample, norm_mix, w_in, norm_q, w_uq, norm_kv, w_uk, w_uv, w_o_attn, ssm_a_re, ssm_a_im, ssm_log_step, ssm_b_re, ssm_b_im, ssm_c_re, ssm_c_im, ssm_d, w_glu, w_out, norm_mlp, w_mlp_in, w_mlp_out, norm_ple, w_ple_gate, w_ple, norm_final):
    depth = w_in.shape[0]
    d_model = x_prompt.shape[-1]
    layers = []
    for i in range(depth):
        ssm_w = (ssm_a_re[i], ssm_a_im[i], ssm_log_step[i], ssm_b_re[i], ssm_b_im[i], ssm_c_re[i], ssm_c_im[i], ssm_d[i])
        layers.append(_layer_weights(w_in[i], norm_q[i], w_uq[i], norm_kv[i], w_uk[i], w_uv[i], w_o_attn[i],
                                     ssm_w, w_glu[i], d_model))
    outs = []
    for x, p in ((x_prompt, p_prompt), (x_sample, p_sample)):
        batch, seq_len, _ = x.shape
        h = x.reshape(batch * seq_len, d_model)
        for i in range(depth):
            final = i == depth - 1
            h = _encoder_layer(h, p[i].reshape(batch * seq_len, -1), layers[i], norm_mix[i], norm_q[i], norm_kv[i],
                               w_out[i], norm_mlp[i], w_mlp_in[i], w_mlp_out[i], norm_ple[i], w_ple_gate[i], w_ple[i],
                               norm_final if final else norm_ple[i], final, batch, seq_len)
        outs.append(h.reshape(batch, seq_len, d_model))
    return tuple(outs)
```

```python
import functools
import math

import jax
import jax.numpy as jnp
from jax import lax
from jax.experimental import pallas as pl
from jax.experimental.pallas import tpu as pltpu

NORM_EPS = 1e-6
ROPE_THETA = 10000.0
LANES = 128
SUBLANES = 8
BF16_ROWS = 16
FLASH_TQ = 2048
FLASH_TK = 1024
FLASH_UNROLL = 2
MXU_DIM = 256
VMEM_LIMIT = 56 * 1024 * 1024
SSM_CHUNK = 32
BF16 = jnp.bfloat16
F32 = jnp.float32


def _params(n_axes, n_parallel=None):
    n_parallel = n_axes if n_parallel is None else n_parallel
    sem = ("parallel",) * n_parallel + ("arbitrary",) * (n_axes - n_parallel)
    return pltpu.CompilerParams(dimension_semantics=sem, vmem_limit_bytes=VMEM_LIMIT)


def _resident(shape):
    nd = len(shape)
    return pl.BlockSpec(shape, lambda *_: (0,) * nd, pipeline_mode=pl.Buffered(1))


def _rms(x, g):
    r = lax.rsqrt(jnp.mean(x * x, axis=-1, keepdims=True) + NORM_EPS)
    return (x * r) * g


def _dot(a, b):
    return jnp.dot(a, b, preferred_element_type=F32)


def _dot_nt(a, b):
    return lax.dot_general(a, b, (((1,), (1,)), ((), ())), preferred_element_type=F32)


def _in_proj_kernel(x_ref, g_ref, wq_ref, wkv_ref, wkr_ref, wu_ref, gq_ref, gkv_ref, cos_ref, sin_ref,
                    xn_ref, qn_ref, ckv_ref, kr_ref, u_ref):
    xn = _rms(x_ref[...], g_ref[...]).astype(BF16)
    xn_ref[...] = xn
    qn_ref[...] = _rms(_dot(xn, wq_ref[...]), gq_ref[...]).astype(BF16)
    ckv_ref[...] = _rms(_dot(xn, wkv_ref[...]), gkv_ref[...]).astype(BF16)
    kr = _dot(xn, wkr_ref[...])
    kr_ref[...] = (kr * cos_ref[...] + pltpu.roll(kr, LANES // 2, 1) * sin_ref[...]).astype(BF16)
    u_ref[...] = _dot(xn, wu_ref[...])


def in_proj(x, g, wq, wkv, wkr, wu, gq, gkv, cos_l, sin_l, seq_len, tm):
    t, d = x.shape
    cq, ckv, sw = wq.shape[1], wkv.shape[1], wu.shape[1]
    npos = seq_len // tm
    row = lambda i: (i, 0)
    pos = lambda i: (i % npos, 0)
    widths = (d, cq, ckv, LANES, sw)
    return pl.pallas_call(
        _in_proj_kernel,
        grid=(t // tm,),
        in_specs=[pl.BlockSpec((tm, d), row), _resident((1, d)),
                  _resident((d, cq)), _resident((d, ckv)), _resident((d, LANES)), _resident((d, sw)),
                  _resident((1, cq)), _resident((1, ckv)),
                  pl.BlockSpec((tm, LANES), pos), pl.BlockSpec((tm, LANES), pos)],
        out_specs=[pl.BlockSpec((tm, w), row) for w in widths],
        out_shape=[jax.ShapeDtypeStruct((t, w), BF16) for w in widths[:-1]] + [jax.ShapeDtypeStruct((t, sw), F32)],
        compiler_params=_params(1),
        name="in_proj",
    )(x, g.reshape(1, d), wq, wkv, wkr, wu, gq.reshape(1, cq), gkv.reshape(1, ckv), cos_l, sin_l)


def _attn_prep_kernel(qn_ref, ckv_ref, kr_ref, wuq_ref, wuk_ref, wuv_ref, cos_ref, sin_ref,
                      qt_ref, k_ref, vt_ref, *, n_heads, dn, dv):
    qn = qn_ref[...]
    ckv = ckv_ref[...]
    cos_t = cos_ref[...]
    sin_t = sin_ref[...]
    half = LANES // 2
    for h in range(n_heads):
        q = _dot_nt(wuq_ref[h], qn)
        qr = q[dn:]
        qr_sw = jnp.concatenate([qr[half:], qr[:half]], axis=0)
        qt_ref[0, h, :dn, :] = q[:dn].astype(BF16)
        qt_ref[0, h, dn:, :] = (qr * cos_t + qr_sw * sin_t).astype(BF16)
    k_all = _dot(ckv, wuk_ref[...])
    kr = kr_ref[...]
    for h in range(n_heads):
        k_ref[0, h, :, :dn] = k_all[:, h * dn:(h + 1) * dn].astype(BF16)
        k_ref[0, h, :, dn:] = kr
    vt = _dot_nt(wuv_ref[...], ckv).astype(BF16)
    tm = vt.shape[1]
    ones_rows = (lax.broadcasted_iota(jnp.int32, (BF16_ROWS, tm), 0) == 0).astype(BF16)
    dve = dv + BF16_ROWS
    for h in range(n_heads):
        vt_ref[0, h * dve:h * dve + dv, :] = vt[h * dv:(h + 1) * dv]
        vt_ref[0, h * dve + dv:(h + 1) * dve, :] = ones_rows


def attn_prep(qn, ckv, kr, wuq_t, wuk, wuv_t, cos_t, sin_t, batch, seq_len, tm):
    n_heads, dk, cq = wuq_t.shape
    dn = dk - LANES
    dv = wuv_t.shape[0] // n_heads
    hdv = n_heads * (dv + BF16_ROWS)
    ckv_dim = ckv.shape[1]
    npos = seq_len // tm
    row = lambda b, i: (b * npos + i, 0)
    col = lambda b, i: (0, i)
    return pl.pallas_call(
        functools.partial(_attn_prep_kernel, n_heads=n_heads, dn=dn, dv=dv),
        grid=(batch, npos),
        in_specs=[pl.BlockSpec((tm, cq), row), pl.BlockSpec((tm, ckv_dim), row), pl.BlockSpec((tm, LANES), row),
                  _resident((n_heads, dk, cq)), _resident((ckv_dim, n_heads * dn)), _resident(wuv_t.shape),
                  pl.BlockSpec((LANES, tm), col), pl.BlockSpec((LANES, tm), col)],
        out_specs=[pl.BlockSpec((1, n_heads, dk, tm), lambda b, i: (b, 0, 0, i)),
                   pl.BlockSpec((1, n_heads, tm, dk), lambda b, i: (b, 0, i, 0)),
                   pl.BlockSpec((1, hdv, tm), lambda b, i: (b, 0, i))],
        out_shape=[jax.ShapeDtypeStruct((batch, n_heads, dk, seq_len), BF16),
                   jax.ShapeDtypeStruct((batch, n_heads, seq_len, dk), BF16),
                   jax.ShapeDtypeStruct((batch, hdv, seq_len), BF16)],
        compiler_params=_params(2),
        name="attn_prep",
    )(qn, ckv, kr, wuq_t, wuk, wuv_t, cos_t, sin_t)


def _flash_kernel(qt_ref, k_ref, vt_ref, o_ref, m_ref, acc_ref, s0_ref, s1_ref, c0_ref, c1_ref,
                  *, tk, nk, dv, unroll):
    qt = qt_ref[0, 0]
    tq = qt.shape[1]
    m_ref[...] = jnp.full(m_ref.shape, -jnp.inf, F32)
    acc_ref[...] = jnp.zeros(acc_ref.shape, F32)

    col_blocks = [slice(c, c + MXU_DIM) for c in range(0, tq, MXU_DIM)]

    def scores(j, buf, cols):
        s_ref, c_ref = buf
        off = pl.multiple_of(j * tk, tk)
        s = _dot(k_ref[0, 0, pl.ds(off, tk), :], qt[:, cols])
        s_ref[:, cols] = s
        c_ref[:, cols] = jnp.max(s, axis=0, keepdims=True)

    def consume(j, buf, cols):
        s_ref, c_ref = buf
        off = pl.multiple_of(j * tk, tk)
        m_prev = m_ref[:, cols]
        m_new = jnp.maximum(m_prev, c_ref[:, cols])
        alpha = jnp.exp2(m_prev - m_new)
        p = jnp.exp2(s_ref[:, cols] - m_new).astype(BF16)
        acc_ref[:, cols] = alpha * acc_ref[:, cols] + _dot(vt_ref[0, :, pl.ds(off, tk)], p)
        m_ref[:, cols] = m_new

    bufs = ((s0_ref, c0_ref), (s1_ref, c1_ref))
    for cols in col_blocks:
        scores(0, bufs[0], cols)

    def group(base, last):
        for u in range(unroll):
            for cols in col_blocks:
                if not (last and u == unroll - 1):
                    scores(base + u + 1, bufs[(u + 1) % 2], cols)
                consume(base + u, bufs[u % 2], cols)

    def body(i, carry):
        group(i * unroll, False)
        return carry

    lax.fori_loop(0, nk // unroll - 1, body, 0)
    group(nk - unroll, True)
    acc = acc_ref[...]
    o_ref[0] = (acc[:dv] / acc[dv:dv + 1]).T.astype(BF16)


def flash_attn(qt, k, vt, dv, tq, tk):
    batch, n_heads, dk, seq_len = qt.shape
    dve = vt.shape[1] // n_heads
    nk = seq_len // tk
    unroll = min(FLASH_UNROLL, nk)
    assert unroll % 2 == 0 and nk % unroll == 0
    return pl.pallas_call(
        functools.partial(_flash_kernel, tk=tk, nk=nk, dv=dv, unroll=unroll),
        grid=(batch, n_heads, seq_len // tq),
        in_specs=[pl.BlockSpec((1, 1, dk, tq), lambda b, h, i: (b, h, 0, i)),
                  pl.BlockSpec((1, 1, seq_len, dk), lambda b, h, i: (b, h, 0, 0)),
                  pl.BlockSpec((1, dve, seq_len), lambda b, h, i: (b, h, 0))],
        out_specs=pl.BlockSpec((1, tq, dv), lambda b, h, i: (b, i, h)),
        out_shape=jax.ShapeDtypeStruct((batch, seq_len, n_heads * dv), BF16),
        scratch_shapes=[pltpu.VMEM((1, tq), F32), pltpu.VMEM((dve, tq), F32),
                        pltpu.VMEM((tk, tq), F32), pltpu.VMEM((tk, tq), F32),
                        pltpu.VMEM((1, tq), F32), pltpu.VMEM((1, tq), F32)],
        compiler_params=_params(3),
        name="flash_attn",
    )(qt, k, vt)


def _rows_to_block(rows):
    n = rows[0].shape[1]
    rid = lax.broadcasted_iota(jnp.int32, (SUBLANES, n), 0)
    blk = jnp.broadcast_to(rows[-1], (SUBLANES, n))
    for k in range(SUBLANES - 2, -1, -1):
        blk = jnp.where(rid == k, rows[k], blk)
    return blk


def _ssm_kernel(u_ref, toep_ref, w1_ref, w2_ref, dec_ref, y_ref, s_ref, xf_ref, xb_ref, *, n_seq, n_chunk, n_state):
    u = u_ref[0]
    s_ref[...] = _dot(u, w1_ref[0])
    dec = dec_ref[0]
    a_re, a_im = dec[0:1], dec[1:2]
    fwd_lane = lax.broadcasted_iota(jnp.int32, (1, LANES), 1) < n_state
    n_blk = n_chunk // SUBLANES

    def step(i, carry):
        new = []
        for q in range(n_seq):
            x_re, x_im = carry[2 * q], carry[2 * q + 1]
            rf = pl.multiple_of(q * n_chunk + i * SUBLANES, SUBLANES)
            rb = pl.multiple_of(q * n_chunk + (n_blk - 1 - i) * SUBLANES, SUBLANES)
            sf = s_ref[pl.ds(rf, SUBLANES), :]
            sb = s_ref[pl.ds(rb, SUBLANES), :]
            rows_re, rows_im = [], []
            for k in range(SUBLANES):
                kb = SUBLANES - 1 - k
                rows_re.append(x_re)
                rows_im.append(x_im)
                s_re = jnp.where(fwd_lane, sf[k:k + 1, :LANES], sb[kb:kb + 1, :LANES])
                s_im = jnp.where(fwd_lane, sf[k:k + 1, LANES:], sb[kb:kb + 1, LANES:])
                x_re, x_im = a_re * x_re - a_im * x_im + s_re, a_re * x_im + a_im * x_re + s_im
            xf_ref[pl.ds(rf, SUBLANES), :LANES] = _rows_to_block(rows_re)
            xf_ref[pl.ds(rf, SUBLANES), LANES:] = _rows_to_block(rows_im)
            xb_ref[pl.ds(rb, SUBLANES), :LANES] = _rows_to_block(rows_re[::-1])
            xb_ref[pl.ds(rb, SUBLANES), LANES:] = _rows_to_block(rows_im[::-1])
            new += [x_re, x_im]
        return tuple(new)

    zero = jnp.zeros((1, LANES), F32)
    lax.fori_loop(0, n_blk, step, (zero,) * (2 * n_seq))
    y_ref[0] = (_dot(u, toep_ref[0]) + _dot_nt(xf_ref[...].astype(BF16), w2_ref[0, 0])
                + _dot_nt(xb_ref[...].astype(BF16), w2_ref[0, 1])).astype(y_ref.dtype)


def ssm(u_rows, toep, w1, w2, dec, n_seq, n_chunk):
    groups, rows, tp = u_rows.shape
    n_state = w1.shape[2] // 4
    assert 2 * n_state == LANES and n_chunk % SUBLANES == 0
    g3 = lambda g: (g, 0, 0)
    return pl.pallas_call(
        functools.partial(_ssm_kernel, n_seq=n_seq, n_chunk=n_chunk, n_state=n_state),
        grid=(groups,),
        in_specs=[pl.BlockSpec((1, rows, tp), g3), pl.BlockSpec((1, tp, tp), g3),
                  pl.BlockSpec((1, tp, 4 * n_state), g3), pl.BlockSpec((1, 2, tp, 4 * n_state), lambda g: (g, 0, 0, 0)),
                  pl.BlockSpec((1, 2, LANES), g3)],
        out_specs=pl.BlockSpec((1, rows, tp), g3),
        out_shape=jax.ShapeDtypeStruct((groups, rows, tp), BF16),
        scratch_shapes=[pltpu.VMEM((rows, 4 * n_state), F32)] * 3,
        compiler_params=_params(1),
        name="ssm",
    )(u_rows, toep, w1, w2, dec)


def ssm_operators(a_re, a_im, log_step, b_re, b_im, c_re, c_im, d_skip, chunk):
    _, groups, n_state = a_re.shape
    p = b_re.shape[-1]
    step = jnp.exp(log_step)[..., None]
    mag = jnp.exp(step * a_re)
    ab_re, ab_im = mag * jnp.cos(step * a_im), mag * jnp.sin(step * a_im)
    den = a_re * a_re + a_im * a_im
    nr, ni = ab_re - 1.0, ab_im
    f_re = (nr * a_re + ni * a_im) / den
    f_im = (ni * a_re - nr * a_im) / den
    bb_re = f_re[..., None] * b_re - f_im[..., None] * b_im
    bb_im = f_re[..., None] * b_im + f_im[..., None] * b_re
    j = jnp.arange(chunk + 1, dtype=F32)[:, None, None, None]
    pw_mag = jnp.exp(j * (step * a_re)[None])
    pw_re = pw_mag * jnp.cos(j * (step * a_im)[None])
    pw_im = pw_mag * jnp.sin(j * (step * a_im)[None])
    def fb(x_f, x_b):
        return jnp.concatenate([x_f, x_b], axis=-1).transpose(1, 0, 2)

    def re_im(f):
        return jnp.stack([f(pw_re), f(pw_im)], axis=1)

    zero = jnp.zeros((chunk - 1, groups, n_state), F32)
    pw_lag = re_im(lambda w: fb(jnp.concatenate([zero, w[:chunk, 0]]), jnp.concatenate([w[chunk - 1::-1, 1], zero])))
    pw_lag = jnp.pad(pw_lag, ((0, 0), (0, 0), (0, 1), (0, 0)))
    pw_in = re_im(lambda w: fb(w[chunk - 1::-1, 0], w[:chunk, 1]))
    pw_out = re_im(lambda w: fb(w[1:chunk + 1, 0], w[chunk:0:-1, 1]))
    dec = jnp.stack([fb(pw_re[chunk:, 0], pw_re[chunk:, 1])[:, 0], fb(pw_im[chunk:, 0], pw_im[chunk:, 1])[:, 0]],
                    axis=1)
    bt = jnp.stack([jnp.concatenate([bb_re[0], bb_re[1]], axis=1), jnp.concatenate([bb_im[0], bb_im[1]], axis=1)],
                   axis=1).transpose(0, 1, 3, 2)
    cc = jnp.stack([jnp.concatenate([c_re[0], c_re[1]], axis=-1), jnp.concatenate([c_im[0], c_im[1]], axis=-1)],
                   axis=1)
    eye = jnp.eye(p, LANES, dtype=F32)
    skip = d_skip.reshape(groups, p)[:, :, None] * eye
    n_rows = -(-(2 * chunk - 1) * p // LANES) * LANES
    sel = jnp.zeros((n_rows, LANES), F32).at[(chunk - 1) * p:chunk * p].set(eye)
    toep, w1, w2 = ssm_tables(bt, cc, skip, sel.astype(BF16), pw_lag, pw_in, pw_out, chunk)
    return toep, w1, w2, dec


def _ssm_tables_kernel(bt_ref, c_ref, skip_ref, sel_ref, pwl_ref, pwi_ref, pwo_ref, toep_ref, w1_ref, w2_ref, rt_ref,
                       *, chunk, n_state):
    p = bt_ref.shape[2]
    bt_re, bt_im = bt_ref[0, 0], bt_ref[0, 1]
    c_re, c_im = c_ref[0, 0], c_ref[0, 1]
    rt_ref[...] = jnp.zeros(rt_ref.shape, BF16)
    for li in range(2 * chunk - 1):
        a_r, a_i = pwl_ref[0, 0, li:li + 1, :], pwl_ref[0, 1, li:li + 1, :]
        rt_ref[li * p:(li + 1) * p, :LANES] = (c_re * a_r - c_im * a_i).astype(BF16)
        rt_ref[li * p:(li + 1) * p, LANES:] = (c_re * a_i + c_im * a_r).astype(BF16)
    lhs = jnp.concatenate([bt_re, -bt_im], axis=1).astype(BF16)
    by_lag = _dot_nt(lhs, rt_ref[...]) + _dot_nt(skip_ref[0].astype(BF16), sel_ref[...])
    width = by_lag.shape[1]
    for s in range(chunk):
        off = (chunk - 1 - s) * p
        win = by_lag if off == 0 else pltpu.roll(by_lag, width - off, 1)
        toep_ref[0, s * p:(s + 1) * p, :] = win[:, :chunk * p].astype(BF16)
    fwd = lax.broadcasted_iota(jnp.int32, (p, LANES), 1) < n_state
    for s in range(chunk):
        rows = slice(s * p, (s + 1) * p)
        a_r, a_i = pwi_ref[0, 0, s:s + 1, :], pwi_ref[0, 1, s:s + 1, :]
        w1_ref[0, rows, :LANES] = (bt_re * a_r - bt_im * a_i).astype(BF16)
        w1_ref[0, rows, LANES:] = (bt_re * a_i + bt_im * a_r).astype(BF16)
        a_r, a_i = pwo_ref[0, 0, s:s + 1, :], pwo_ref[0, 1, s:s + 1, :]
        o_re = c_re * a_r - c_im * a_i
        o_im = -(c_re * a_i + c_im * a_r)
        w2_ref[0, 0, rows, :LANES] = jnp.where(fwd, o_re, 0.0).astype(BF16)
        w2_ref[0, 0, rows, LANES:] = jnp.where(fwd, o_im, 0.0).astype(BF16)
        w2_ref[0, 1, rows, :LANES] = jnp.where(fwd, 0.0, o_re).astype(BF16)
        w2_ref[0, 1, rows, LANES:] = jnp.where(fwd, 0.0, o_im).astype(BF16)


def ssm_tables(bt, cc, skip, sel, pw_lag, pw_in, pw_out, chunk):
    groups, _, p, lanes = bt.shape
    n_state = lanes // 2
    tp = chunk * p
    g4 = lambda g: (g, 0, 0, 0)
    blk4 = lambda a: pl.BlockSpec((1,) + a.shape[1:], g4)
    return pl.pallas_call(
        functools.partial(_ssm_tables_kernel, chunk=chunk, n_state=n_state),
        grid=(groups,),
        in_specs=[blk4(bt), blk4(cc), pl.BlockSpec((1, p, LANES), lambda g: (g, 0, 0)), _resident(sel.shape),
                  blk4(pw_lag), blk4(pw_in), blk4(pw_out)],
        out_specs=[pl.BlockSpec((1, tp, tp), lambda g: (g, 0, 0)), pl.BlockSpec((1, tp, 2 * lanes), lambda g: (g, 0, 0)),
                   pl.BlockSpec((1, 2, tp, 2 * lanes), g4)],
        out_shape=[jax.ShapeDtypeStruct((groups, tp, tp), BF16), jax.ShapeDtypeStruct((groups, tp, 2 * lanes), BF16),
                   jax.ShapeDtypeStruct((groups, 2, tp, 2 * lanes), BF16)],
        scratch_shapes=[pltpu.VMEM(sel.shape[:1] + (2 * lanes,), BF16)],
        compiler_params=_params(1),
        name="ssm_tables",
    )(bt, cc, skip, sel, pw_lag, pw_in, pw_out)


def lane_select(p):
    per = LANES // p
    r = jnp.arange(per * LANES)
    c = jnp.arange(2 * LANES)
    a, b, x = r // LANES, (r % LANES) // p, r % p
    h, a2, x2 = c // LANES, (c % LANES) // p, c % p
    k = jnp.arange(per // 2)[:, None, None]
    sel = (b[None, :, None] == 2 * k + h[None, None, :]) & (a[:, None] == a2[None, :]) & (x[:, None] == x2[None, :])
    return sel.astype(BF16)


def _to_rows_kernel(u_ref, sel_ref, o_ref, *, chunk):
    gps, rows, _ = o_ref.shape
    per = sel_ref.shape[1] // LANES
    for cb in range(chunk // per):
        cat = jnp.concatenate([u_ref[pl.ds(cb * per + s, rows, stride=chunk), :] for s in range(per)], axis=1)
        cat = cat.astype(BF16)
        for k in range(gps // 2):
            two = _dot(cat, sel_ref[k])
            o_ref[2 * k, :, cb * LANES:(cb + 1) * LANES] = two[:, :LANES].astype(o_ref.dtype)
            o_ref[2 * k + 1, :, cb * LANES:(cb + 1) * LANES] = two[:, LANES:].astype(o_ref.dtype)


def to_chunk_rows(u, sel, p, chunk, rows):
    t, width = u.shape
    gps = LANES // p
    return pl.pallas_call(
        functools.partial(_to_rows_kernel, chunk=chunk),
        grid=(width // LANES, t // (rows * chunk)),
        in_specs=[pl.BlockSpec((rows * chunk, LANES), lambda g, i: (i, g)), _resident(sel.shape)],
        out_specs=pl.BlockSpec((gps, rows, chunk * p), lambda g, i: (g, i, 0)),
        out_shape=jax.ShapeDtypeStruct((width // p, t // chunk, chunk * p), BF16),
        compiler_params=_params(2),
        name="to_chunk_rows",
    )(u, sel)


def _from_rows_kernel(y_ref, sel_ref, o_ref, *, chunk):
    gps, rows, _ = y_ref.shape
    per = sel_ref.shape[1] // LANES
    for cb in range(chunk // per):
        cat = jnp.concatenate([y_ref[g, :, cb * LANES:(cb + 1) * LANES] for g in range(gps)], axis=1)
        for k in range(per // 2):
            two = _dot(cat, sel_ref[k])
            o_ref[pl.ds(cb * per + 2 * k, rows, stride=chunk), :] = two[:, :LANES]
            o_ref[pl.ds(cb * per + 2 * k + 1, rows, stride=chunk), :] = two[:, LANES:]


def from_chunk_rows(y_rows, sel, p, chunk, rows):
    groups, n_rows, _ = y_rows.shape
    gps = LANES // p
    return pl.pallas_call(
        functools.partial(_from_rows_kernel, chunk=chunk),
        grid=(groups // gps, n_rows // rows),
        in_specs=[pl.BlockSpec((gps, rows, chunk * p), lambda g, i: (g, i, 0)), _resident(sel.shape)],
        out_specs=pl.BlockSpec((rows * chunk, LANES), lambda g, i: (i, g)),
        out_shape=jax.ShapeDtypeStruct((n_rows * chunk, groups * p), F32),
        compiler_params=_params(2),
        name="from_chunk_rows",
    )(y_rows, sel)


def _mix_kernel(xn_ref, o_ref, y_ref, wga_ref, wgs_ref, wo_ref, wz1_ref, wz2_ref, mix_ref, gy_ref):
    @pl.when(pl.program_id(1) == 0)
    def _():
        gy_ref[...] = jax.nn.gelu(y_ref[...].astype(F32)).astype(BF16)

    xn = xn_ref[...]
    gy = gy_ref[...]
    o_a = _dot(o_ref[...], wo_ref[...])
    o_s = _dot(gy, wz1_ref[...]) * jax.nn.sigmoid(_dot(gy, wz2_ref[...]))
    g_a = jax.nn.sigmoid(_dot(xn, wga_ref[...]))
    g_s = jax.nn.sigmoid(_dot(xn, wgs_ref[...]))
    mix_ref[...] = (g_a * o_a + g_s * o_s).astype(BF16)


def mix_merge(xn, o, y, wga, wgs, wo, wz1, wz2, tm, tn):
    t, d = xn.shape
    ho, sw = o.shape[1], y.shape[1]
    row = lambda i, j: (i, 0)
    col = lambda i, j: (0, j)
    return pl.pallas_call(
        _mix_kernel,
        grid=(t // tm, d // tn),
        in_specs=[pl.BlockSpec((tm, d), row), pl.BlockSpec((tm, ho), row), pl.BlockSpec((tm, sw), row),
                  pl.BlockSpec((d, tn), col), pl.BlockSpec((d, tn), col), pl.BlockSpec((ho, tn), col),
                  pl.BlockSpec((sw, tn), col), pl.BlockSpec((sw, tn), col)],
        out_specs=pl.BlockSpec((tm, tn), lambda i, j: (i, j)),
        out_shape=jax.ShapeDtypeStruct((t, d), BF16),
        scratch_shapes=[pltpu.VMEM((tm, sw), BF16)],
        compiler_params=_params(2, 1),
        name="mix_merge",
    )(xn, o, y, wga, wgs, wo, wz1, wz2)


def _resid_norm_kernel(x_ref, mix_ref, w_ref, g_ref, h_ref, n_ref):
    h = x_ref[...] + _dot(mix_ref[...], w_ref[...])
    h_ref[...] = h
    n_ref[...] = _rms(h, g_ref[...]).astype(BF16)


def resid_norm(x, mix, w, g, tm):
    t, d = x.shape
    row = lambda i: (i, 0)
    return pl.pallas_call(
        _resid_norm_kernel,
        grid=(t // tm,),
        in_specs=[pl.BlockSpec((tm, d), row), pl.BlockSpec((tm, d), row), _resident((d, d)), _resident((1, d))],
        out_specs=[pl.BlockSpec((tm, d), row), pl.BlockSpec((tm, d), row)],
        out_shape=[jax.ShapeDtypeStruct((t, d), F32), jax.ShapeDtypeStruct((t, d), BF16)],
        compiler_params=_params(1),
        name="resid_norm",
    )(x, mix, w, g.reshape(1, d))


def _mlp_kernel(n_ref, h_ref, w1_ref, w2_ref, o_ref, acc_ref):
    j = pl.program_id(1)

    @pl.when(j == 0)
    def _():
        acc_ref[...] = h_ref[...]

    a = jnp.maximum(_dot(n_ref[...], w1_ref[...]), 0.0)
    acc_ref[...] += _dot((a * a).astype(BF16), w2_ref[...])

    @pl.when(j == pl.num_programs(1) - 1)
    def _():
        o_ref[...] = acc_ref[...]


def mlp(n, h, w1, w2, tm, tf):
    t, d = h.shape
    ff = w1.shape[1]
    row = lambda i, j: (i, 0)
    return pl.pallas_call(
        _mlp_kernel,
        grid=(t // tm, ff // tf),
        in_specs=[pl.BlockSpec((tm, d), row), pl.BlockSpec((tm, d), row),
                  pl.BlockSpec((d, tf), lambda i, j: (0, j)), pl.BlockSpec((tf, d), lambda i, j: (j, 0))],
        out_specs=pl.BlockSpec((tm, d), row),
        out_shape=jax.ShapeDtypeStruct((t, d), F32),
        scratch_shapes=[pltpu.VMEM((tm, d), F32)],
        compiler_params=_params(2, 1),
        name="mlp",
    )(n, h, w1, w2)


def _ple_kernel(h_ref, p_ref, wg_ref, wp_ref, gp_ref, gf_ref, o_ref, *, final):
    h = h_ref[...]
    gate = jax.nn.sigmoid(_dot(_rms(h, gp_ref[...]).astype(BF16), wg_ref[...]))
    h = h + gate * _dot(p_ref[...].astype(BF16), wp_ref[...])
    o_ref[...] = _rms(h, gf_ref[...]) if final else h


def ple(h, p, wg, wp, gp, gf, tm, final):
    t, d = h.shape
    pd = p.shape[1]
    row = lambda i: (i, 0)
    return pl.pallas_call(
        functools.partial(_ple_kernel, final=final),
        grid=(t // tm,),
        in_specs=[pl.BlockSpec((tm, d), row), pl.BlockSpec((tm, pd), row), _resident((d, d)), _resident((pd, d)),
                  _resident((1, d)), _resident((1, d))],
        out_specs=pl.BlockSpec((tm, d), row),
        out_shape=jax.ShapeDtypeStruct((t, d), F32),
        compiler_params=_params(1),
        name="ple",
    )(h, p, wg, wp, gp.reshape(1, d), gf.reshape(1, d))


def _rope_tables(length, rope_dim):
    pos = jnp.arange(length, dtype=F32)
    inv = ROPE_THETA ** (-jnp.arange(0, rope_dim, 2, dtype=F32) / rope_dim)
    ang = pos[:, None] * inv[None, :]
    return jnp.cos(ang), jnp.sin(ang)


def _rope_lanes(cos, sin):
    length, half = cos.shape
    z = jnp.zeros((length, LANES // 2 - half), F32)
    return (jnp.concatenate([cos, z, cos, z], axis=1), jnp.concatenate([-sin, z, sin, z], axis=1))


def _spread_rope(w, half):
    z = jnp.zeros(w.shape[:-1] + (LANES // 2 - half,), w.dtype)
    return jnp.concatenate([w[..., :half], z, w[..., half:], z], axis=-1)


def _tile(n, pref):
    t = min(n, pref)
    while n % t:
        t //= 2
    return t


def _layer_weights(w_in, norm_q, w_uq, norm_kv, w_uk, w_uv, w_o_attn, ssm, w_glu, d_model):
    cq, ckv = norm_q.shape[0], norm_kv.shape[0]
    n_heads, dn = w_uk.shape[1], w_uk.shape[2]
    dv = w_uv.shape[2]
    sw = ssm[7].shape[0]
    rope = w_in.shape[1] - (cq + ckv + sw + 2 * d_model)
    half = rope // 2
    o_q, o_kv, o_kr, o_u, o_ga, o_gs = 0, cq, cq + ckv, cq + ckv + rope, cq + ckv + rope + sw, cq + ckv + rope + sw + d_model
    scale = (dn + rope) ** -0.5 * math.log2(math.e)
    wuq = w_uq.reshape(cq, n_heads, dn + rope) * scale
    wuq_t = jnp.concatenate([wuq[..., :dn], _spread_rope(wuq[..., dn:], half)], axis=-1).transpose(1, 2, 0)
    return dict(
        rope=rope, n_heads=n_heads, dn=dn, dv=dv,
        wq=w_in[:, o_q:o_kv].astype(BF16), wkv=w_in[:, o_kv:o_kr].astype(BF16),
        wkr=_spread_rope(w_in[:, o_kr:o_u], half).astype(BF16),
        wu=w_in[:, o_u:o_ga].astype(BF16), wga=w_in[:, o_ga:o_gs].astype(BF16), wgs=w_in[:, o_gs:].astype(BF16),
        wuq_t=wuq_t.astype(BF16),
        wuk=w_uk.reshape(ckv, n_heads * dn).astype(BF16),
        wuv_t=w_uv.reshape(ckv, n_heads * dv).T.astype(BF16),
        wo=w_o_attn.astype(BF16),
        wz1=w_glu[:, :d_model].astype(BF16), wz2=w_glu[:, d_model:].astype(BF16),
        ssm_ops=ssm_operators(*ssm, SSM_CHUNK),
        lane_sel=lane_select(ssm[3].shape[-1]),
    )


def _encoder_layer(h, p, lw, norm_mix, norm_q, norm_kv, w_out, norm_mlp, w_mlp_in, w_mlp_out,
                   norm_ple, w_ple_gate, w_ple, norm_out, final, batch, seq_len):
    t, d = h.shape
    tm = _tile(seq_len, 512)
    cos, sin = _rope_tables(seq_len, lw['rope'])
    cos_l, sin_l = _rope_lanes(cos, sin)
    xn, qn, ckv, kr, u = in_proj(h, norm_mix, lw['wq'], lw['wkv'], lw['wkr'], lw['wu'], norm_q, norm_kv,
                                 cos_l, sin_l, seq_len, tm)
    qt, k, vt = attn_prep(qn, ckv, kr, lw['wuq_t'], lw['wuk'], lw['wuv_t'], cos_l.T, sin_l.T, batch, seq_len, tm)
    o = flash_attn(qt, k, vt, lw['dv'], _tile(seq_len, FLASH_TQ), _tile(seq_len, FLASH_TK)).reshape(t, -1)
    toep, w1, w2, dec = lw['ssm_ops']
    groups = toep.shape[0]
    gp = lw['wu'].shape[1] // groups
    n_chunk = seq_len // SSM_CHUNK
    rows_tile = _tile(batch * n_chunk, 64)
    u_rows = to_chunk_rows(u, lw['lane_sel'], gp, SSM_CHUNK, rows_tile)
    y_rows = ssm(u_rows, toep, w1, w2, dec, batch, n_chunk)
    y = from_chunk_rows(y_rows, lw['lane_sel'], gp, SSM_CHUNK, rows_tile)
    mix = mix_merge(xn, o, y, lw['wga'], lw['wgs'], lw['wo'], lw['wz1'], lw['wz2'], tm, _tile(d, 512))
    h1, n1 = resid_norm(h, mix, w_out.astype(BF16), norm_mlp, tm)
    h2 = mlp(n1, h1, w_mlp_in.astype(BF16), w_mlp_out.astype(BF16), tm, _tile(w_mlp_in.shape[1], 1024))
    return ple(h2, p, w_ple_gate.astype(BF16), w_ple.astype(BF16), norm_ple, norm_out, tm, final)


def kernel(x_prompt, x_sample, p_prompt, p_sample, norm_mix, w_in, norm_q, w_uq, norm_kv, w_uk, w_uv, w_o_attn, ssm_a_re, ssm_a_im, ssm_log_step, ssm_b_re, ssm_b_im, ssm_c_re, ssm_c_im, ssm_d, w_glu, w_out, norm_mlp, w_mlp_in, w_mlp_out, norm_ple, w_ple_gate, w_ple, norm_final):
    depth = w_in.shape[0]
    d_model = x_prompt.shape[-1]
    layers = []
    for i in range(depth):
        ssm_w = (ssm_a_re[i], ssm_a_im[i], ssm_log_step[i], ssm_b_re[i], ssm_b_im[i], ssm_c_re[i], ssm_c_im[i], ssm_d[i])
        layers.append(_layer_weights(w_in[i], norm_q[i], w_uq[i], norm_kv[i], w_uk[i], w_uv[i], w_o_attn[i],
                                     ssm_w, w_glu[i], d_model))
    outs = []
    for x, p in ((x_prompt, p_prompt), (x_sample, p_sample)):
        batch, seq_len, _ = x.shape
        h = x.reshape(batch * seq_len, d_model)
        for i in range(depth):
            final = i == depth - 1
            h = _encoder_layer(h, p[i].reshape(batch * seq_len, -1), layers[i], norm_mix[i], norm_q[i], norm_kv[i],
                               w_out[i], norm_mlp[i], w_mlp_in[i], w_mlp_out[i], norm_ple[i], w_ple_gate[i], w_ple[i],
                               norm_final if final else norm_ple[i], final, batch, seq_len)
        outs.append(h.reshape(batch, seq_len, d_model))
    return tuple(outs)
```

```python
import functools
import math

import jax
import jax.numpy as jnp
from jax import lax
from jax.experimental import pallas as pl
from jax.experimental.pallas import tpu as pltpu

NORM_EPS = 1e-6
ROPE_THETA = 10000.0
LANES = 128
SUBLANES = 8
BF16_ROWS = 16
FLASH_TQ = 2048
FLASH_TK = 512
FLASH_UNROLL = 4
MXU_DIM = 256
VMEM_LIMIT = 56 * 1024 * 1024
SSM_CHUNK = 32
BF16 = jnp.bfloat16
F32 = jnp.float32


def _params(n_axes, n_parallel=None):
    n_parallel = n_axes if n_parallel is None else n_parallel
    sem = ("parallel",) * n_parallel + ("arbitrary",) * (n_axes - n_parallel)
    return pltpu.CompilerParams(dimension_semantics=sem, vmem_limit_bytes=VMEM_LIMIT)


def _resident(shape):
    nd = len(shape)
    return pl.BlockSpec(shape, lambda *_: (0,) * nd, pipeline_mode=pl.Buffered(1))


def _rms(x, g):
    r = lax.rsqrt(jnp.mean(x * x, axis=-1, keepdims=True) + NORM_EPS)
    return (x * r) * g


def _dot(a, b):
    return jnp.dot(a, b, preferred_element_type=F32)


def _dot_nt(a, b):
    return lax.dot_general(a, b, (((1,), (1,)), ((), ())), preferred_element_type=F32)


def _in_proj_kernel(x_ref, g_ref, wq_ref, wkv_ref, wkr_ref, wu_ref, gq_ref, gkv_ref, cos_ref, sin_ref,
                    xn_ref, qn_ref, ckv_ref, kr_ref, u_ref):
    xn = _rms(x_ref[...], g_ref[...]).astype(BF16)
    xn_ref[...] = xn
    qn_ref[...] = _rms(_dot(xn, wq_ref[...]), gq_ref[...]).astype(BF16)
    ckv_ref[...] = _rms(_dot(xn, wkv_ref[...]), gkv_ref[...]).astype(BF16)
    kr = _dot(xn, wkr_ref[...])
    kr_ref[...] = (kr * cos_ref[...] + pltpu.roll(kr, LANES // 2, 1) * sin_ref[...]).astype(BF16)
    u_ref[...] = _dot(xn, wu_ref[...])


def in_proj(x, g, wq, wkv, wkr, wu, gq, gkv, cos_l, sin_l, seq_len, tm):
    t, d = x.shape
    cq, ckv, sw = wq.shape[1], wkv.shape[1], wu.shape[1]
    npos = seq_len // tm
    row = lambda i: (i, 0)
    pos = lambda i: (i % npos, 0)
    widths = (d, cq, ckv, LANES, sw)
    return pl.pallas_call(
        _in_proj_kernel,
        grid=(t // tm,),
        in_specs=[pl.BlockSpec((tm, d), row), _resident((1, d)),
                  _resident((d, cq)), _resident((d, ckv)), _resident((d, LANES)), _resident((d, sw)),
                  _resident((1, cq)), _resident((1, ckv)),
                  pl.BlockSpec((tm, LANES), pos), pl.BlockSpec((tm, LANES), pos)],
        out_specs=[pl.BlockSpec((tm, w), row) for w in widths],
        out_shape=[jax.ShapeDtypeStruct((t, w), BF16) for w in widths[:-1]] + [jax.ShapeDtypeStruct((t, sw), F32)],
        compiler_params=_params(1),
        name="in_proj",
    )(x, g.reshape(1, d), wq, wkv, wkr, wu, gq.reshape(1, cq), gkv.reshape(1, ckv), cos_l, sin_l)


def _attn_prep_kernel(qn_ref, ckv_ref, kr_ref, wuq_ref, wuk_ref, wuv_ref, cos_ref, sin_ref,
                      qt_ref, k_ref, vt_ref, *, n_heads, dn, dv):
    qn = qn_ref[...]
    ckv = ckv_ref[...]
    cos_t = cos_ref[...]
    sin_t = sin_ref[...]
    half = LANES // 2
    for h in range(n_heads):
        q = _dot_nt(wuq_ref[h], qn)
        qr = q[dn:]
        qr_sw = jnp.concatenate([qr[half:], qr[:half]], axis=0)
        qt_ref[0, h, :dn, :] = q[:dn].astype(BF16)
        qt_ref[0, h, dn:, :] = (qr * cos_t + qr_sw * sin_t).astype(BF16)
    k_all = _dot(ckv, wuk_ref[...])
    kr = kr_ref[...]
    for h in range(n_heads):
        k_ref[0, h, :, :dn] = k_all[:, h * dn:(h + 1) * dn].astype(BF16)
        k_ref[0, h, :, dn:] = kr
    vt = _dot_nt(wuv_ref[...], ckv).astype(BF16)
    tm = vt.shape[1]
    ones_rows = (lax.broadcasted_iota(jnp.int32, (BF16_ROWS, tm), 0) == 0).astype(BF16)
    dve = dv + BF16_ROWS
    for h in range(n_heads):
        vt_ref[0, h * dve:h * dve + dv, :] = vt[h * dv:(h + 1) * dv]
        vt_ref[0, h * dve + dv:(h + 1) * dve, :] = ones_rows


def attn_prep(qn, ckv, kr, wuq_t, wuk, wuv_t, cos_t, sin_t, batch, seq_len, tm):
    n_heads, dk, cq = wuq_t.shape
    dn = dk - LANES
    dv = wuv_t.shape[0] // n_heads
    hdv = n_heads * (dv + BF16_ROWS)
    ckv_dim = ckv.shape[1]
    npos = seq_len // tm
    row = lambda b, i: (b * npos + i, 0)
    col = lambda b, i: (0, i)
    return pl.pallas_call(
        functools.partial(_attn_prep_kernel, n_heads=n_heads, dn=dn, dv=dv),
        grid=(batch, npos),
        in_specs=[pl.BlockSpec((tm, cq), row), pl.BlockSpec((tm, ckv_dim), row), pl.BlockSpec((tm, LANES), row),
                  _resident((n_heads, dk, cq)), _resident((ckv_dim, n_heads * dn)), _resident(wuv_t.shape),
                  pl.BlockSpec((LANES, tm), col), pl.BlockSpec((LANES, tm), col)],
        out_specs=[pl.BlockSpec((1, n_heads, dk, tm), lambda b, i: (b, 0, 0, i)),
                   pl.BlockSpec((1, n_heads, tm, dk), lambda b, i: (b, 0, i, 0)),
                   pl.BlockSpec((1, hdv, tm), lambda b, i: (b, 0, i))],
        out_shape=[jax.ShapeDtypeStruct((batch, n_heads, dk, seq_len), BF16),
                   jax.ShapeDtypeStruct((batch, n_heads, seq_len, dk), BF16),
                   jax.ShapeDtypeStruct((batch, hdv, seq_len), BF16)],
        compiler_params=_params(2),
        name="attn_prep",
    )(qn, ckv, kr, wuq_t, wuk, wuv_t, cos_t, sin_t)


def _flash_kernel(qt_ref, k_ref, vt_ref, o_ref, m_ref, acc_ref, s0_ref, s1_ref, c0_ref, c1_ref,
                  *, tk, nk, dv, unroll):
    qt = qt_ref[0, 0]
    tq = qt.shape[1]
    m_ref[...] = jnp.full(m_ref.shape, -jnp.inf, F32)
    acc_ref[...] = jnp.zeros(acc_ref.shape, F32)

    col_blocks = [slice(c, c + MXU_DIM) for c in range(0, tq, MXU_DIM)]

    def scores(j, buf, cols):
        s_ref, c_ref = buf
        off = pl.multiple_of(j * tk, tk)
        s = _dot(k_ref[0, 0, pl.ds(off, tk), :], qt[:, cols])
        s_ref[:, cols] = s
        c_ref[:, cols] = jnp.max(s, axis=0, keepdims=True)

    def consume(j, buf, cols):
        s_ref, c_ref = buf
        off = pl.multiple_of(j * tk, tk)
        m_prev = m_ref[:, cols]
        m_new = jnp.maximum(m_prev, c_ref[:, cols])
        alpha = jnp.exp2(m_prev - m_new)
        p = jnp.exp2(s_ref[:, cols] - m_new).astype(BF16)
        acc_ref[:, cols] = alpha * acc_ref[:, cols] + _dot(vt_ref[0, :, pl.ds(off, tk)], p)
        m_ref[:, cols] = m_new

    bufs = ((s0_ref, c0_ref), (s1_ref, c1_ref))
    for cols in col_blocks:
        scores(0, bufs[0], cols)

    def group(base, last):
        for u in range(unroll):
            for cols in col_blocks:
                if not (last and u == unroll - 1):
                    scores(base + u + 1, bufs[(u + 1) % 2], cols)
                consume(base + u, bufs[u % 2], cols)

    def body(i, carry):
        group(i * unroll, False)
        return carry

    lax.fori_loop(0, nk // unroll - 1, body, 0)
    group(nk - unroll, True)
    acc = acc_ref[...]
    o_ref[0] = (acc[:dv] / acc[dv:dv + 1]).T.astype(BF16)


def flash_attn(qt, k, vt, dv, tq, tk):
    batch, n_heads, dk, seq_len = qt.shape
    dve = vt.shape[1] // n_heads
    nk = seq_len // tk
    unroll = min(FLASH_UNROLL, nk)
    assert unroll % 2 == 0 and nk % unroll == 0
    return pl.pallas_call(
        functools.partial(_flash_kernel, tk=tk, nk=nk, dv=dv, unroll=unroll),
        grid=(batch, n_heads, seq_len // tq),
        in_specs=[pl.BlockSpec((1, 1, dk, tq), lambda b, h, i: (b, h, 0, i)),
                  pl.BlockSpec((1, 1, seq_len, dk), lambda b, h, i: (b, h, 0, 0)),
                  pl.BlockSpec((1, dve, seq_len), lambda b, h, i: (b, h, 0))],
        out_specs=pl.BlockSpec((1, tq, dv), lambda b, h, i: (b, i, h)),
        out_shape=jax.ShapeDtypeStruct((batch, seq_len, n_heads * dv), BF16),
        scratch_shapes=[pltpu.VMEM((1, tq), F32), pltpu.VMEM((dve, tq), F32),
                        pltpu.VMEM((tk, tq), F32), pltpu.VMEM((tk, tq), F32),
                        pltpu.VMEM((1, tq), F32), pltpu.VMEM((1, tq), F32)],
        compiler_params=_params(3),
        name="flash_attn",
    )(qt, k, vt)


def _rows_to_block(rows):
    n = rows[0].shape[1]
    rid = lax.broadcasted_iota(jnp.int32, (SUBLANES, n), 0)
    blk = jnp.broadcast_to(rows[-1], (SUBLANES, n))
    for k in range(SUBLANES - 2, -1, -1):
        blk = jnp.where(rid == k, rows[k], blk)
    return blk


def _ssm_kernel(u_ref, toep_ref, w1_ref, w2_ref, dec_ref, y_ref, s_ref, xf_ref, xb_ref, *, n_seq, n_chunk, n_state):
    u = u_ref[0]
    s_ref[...] = _dot(u, w1_ref[0])
    dec = dec_ref[0]
    a_re, a_im = dec[0:1], dec[1:2]
    fwd_lane = lax.broadcasted_iota(jnp.int32, (1, LANES), 1) < n_state
    n_blk = n_chunk // SUBLANES

    def step(i, carry):
        new = []
        for q in range(n_seq):
            x_re, x_im = carry[2 * q], carry[2 * q + 1]
            rf = pl.multiple_of(q * n_chunk + i * SUBLANES, SUBLANES)
            rb = pl.multiple_of(q * n_chunk + (n_blk - 1 - i) * SUBLANES, SUBLANES)
            sf = s_ref[pl.ds(rf, SUBLANES), :]
            sb = s_ref[pl.ds(rb, SUBLANES), :]
            rows_re, rows_im = [], []
            for k in range(SUBLANES):
                kb = SUBLANES - 1 - k
                rows_re.append(x_re)
                rows_im.append(x_im)
                s_re = jnp.where(fwd_lane, sf[k:k + 1, :LANES], sb[kb:kb + 1, :LANES])
                s_im = jnp.where(fwd_lane, sf[k:k + 1, LANES:], sb[kb:kb + 1, LANES:])
                x_re, x_im = a_re * x_re - a_im * x_im + s_re, a_re * x_im + a_im * x_re + s_im
            xf_ref[pl.ds(rf, SUBLANES), :LANES] = _rows_to_block(rows_re)
            xf_ref[pl.ds(rf, SUBLANES), LANES:] = _rows_to_block(rows_im)
            xb_ref[pl.ds(rb, SUBLANES), :LANES] = _rows_to_block(rows_re[::-1])
            xb_ref[pl.ds(rb, SUBLANES), LANES:] = _rows_to_block(rows_im[::-1])
            new += [x_re, x_im]
        return tuple(new)

    zero = jnp.zeros((1, LANES), F32)
    lax.fori_loop(0, n_blk, step, (zero,) * (2 * n_seq))
    y_ref[0] = (_dot(u, toep_ref[0]) + _dot_nt(xf_ref[...].astype(BF16), w2_ref[0, 0])
                + _dot_nt(xb_ref[...].astype(BF16), w2_ref[0, 1])).astype(y_ref.dtype)


def ssm(u_rows, toep, w1, w2, dec, n_seq, n_chunk):
    groups, rows, tp = u_rows.shape
    n_state = w1.shape[2] // 4
    assert 2 * n_state == LANES and n_chunk % SUBLANES == 0
    g3 = lambda g: (g, 0, 0)
    return pl.pallas_call(
        functools.partial(_ssm_kernel, n_seq=n_seq, n_chunk=n_chunk, n_state=n_state),
        grid=(groups,),
        in_specs=[pl.BlockSpec((1, rows, tp), g3), pl.BlockSpec((1, tp, tp), g3),
                  pl.BlockSpec((1, tp, 4 * n_state), g3), pl.BlockSpec((1, 2, tp, 4 * n_state), lambda g: (g, 0, 0, 0)),
                  pl.BlockSpec((1, 2, LANES), g3)],
        out_specs=pl.BlockSpec((1, rows, tp), g3),
        out_shape=jax.ShapeDtypeStruct((groups, rows, tp), BF16),
        scratch_shapes=[pltpu.VMEM((rows, 4 * n_state), F32)] * 3,
        compiler_params=_params(1),
        name="ssm",
    )(u_rows, toep, w1, w2, dec)


def ssm_operators(a_re, a_im, log_step, b_re, b_im, c_re, c_im, d_skip, chunk):
    _, groups, n_state = a_re.shape
    p = b_re.shape[-1]
    step = jnp.exp(log_step)[..., None]
    mag = jnp.exp(step * a_re)
    ab_re, ab_im = mag * jnp.cos(step * a_im), mag * jnp.sin(step * a_im)
    den = a_re * a_re + a_im * a_im
    nr, ni = ab_re - 1.0, ab_im
    f_re = (nr * a_re + ni * a_im) / den
    f_im = (ni * a_re - nr * a_im) / den
    bb_re = f_re[..., None] * b_re - f_im[..., None] * b_im
    bb_im = f_re[..., None] * b_im + f_im[..., None] * b_re
    j = jnp.arange(chunk + 1, dtype=F32)[:, None, None, None]
    pw_mag = jnp.exp(j * (step * a_re)[None])
    pw_re = pw_mag * jnp.cos(j * (step * a_im)[None])
    pw_im = pw_mag * jnp.sin(j * (step * a_im)[None])
    def fb(x_f, x_b):
        return jnp.concatenate([x_f, x_b], axis=-1).transpose(1, 0, 2)

    def re_im(f):
        return jnp.stack([f(pw_re), f(pw_im)], axis=1)

    zero = jnp.zeros((chunk - 1, groups, n_state), F32)
    pw_lag = re_im(lambda w: fb(jnp.concatenate([zero, w[:chunk, 0]]), jnp.concatenate([w[chunk - 1::-1, 1], zero])))
    pw_lag = jnp.pad(pw_lag, ((0, 0), (0, 0), (0, 1), (0, 0)))
    pw_in = re_im(lambda w: fb(w[chunk - 1::-1, 0], w[:chunk, 1]))
    pw_out = re_im(lambda w: fb(w[1:chunk + 1, 0], w[chunk:0:-1, 1]))
    dec = jnp.stack([fb(pw_re[chunk:, 0], pw_re[chunk:, 1])[:, 0], fb(pw_im[chunk:, 0], pw_im[chunk:, 1])[:, 0]],
                    axis=1)
    bt = jnp.stack([jnp.concatenate([bb_re[0], bb_re[1]], axis=1), jnp.concatenate([bb_im[0], bb_im[1]], axis=1)],
                   axis=1).transpose(0, 1, 3, 2)
    cc = jnp.stack([jnp.concatenate([c_re[0], c_re[1]], axis=-1), jnp.concatenate([c_im[0], c_im[1]], axis=-1)],
                   axis=1)
    eye = jnp.eye(p, LANES, dtype=F32)
    skip = d_skip.reshape(groups, p)[:, :, None] * eye
    n_rows = -(-(2 * chunk - 1) * p // LANES) * LANES
    sel = jnp.zeros((n_rows, LANES), F32).at[(chunk - 1) * p:chunk * p].set(eye)
    toep, w1, w2 = ssm_tables(bt, cc, skip, sel.astype(BF16), pw_lag, pw_in, pw_out, chunk)
    return toep, w1, w2, dec


def _ssm_tables_kernel(bt_ref, c_ref, skip_ref, sel_ref, pwl_ref, pwi_ref, pwo_ref, toep_ref, w1_ref, w2_ref, rt_ref,
                       *, chunk, n_state):
    p = bt_ref.shape[2]
    bt_re, bt_im = bt_ref[0, 0], bt_ref[0, 1]
    c_re, c_im = c_ref[0, 0], c_ref[0, 1]
    rt_ref[...] = jnp.zeros(rt_ref.shape, BF16)
    for li in range(2 * chunk - 1):
        a_r, a_i = pwl_ref[0, 0, li:li + 1, :], pwl_ref[0, 1, li:li + 1, :]
        rt_ref[li * p:(li + 1) * p, :LANES] = (c_re * a_r - c_im * a_i).astype(BF16)
        rt_ref[li * p:(li + 1) * p, LANES:] = (c_re * a_i + c_im * a_r).astype(BF16)
    lhs = jnp.concatenate([bt_re, -bt_im], axis=1).astype(BF16)
    by_lag = _dot_nt(lhs, rt_ref[...]) + _dot_nt(skip_ref[0].astype(BF16), sel_ref[...])
    width = by_lag.shape[1]
    for s in range(chunk):
        off = (chunk - 1 - s) * p
        win = by_lag if off == 0 else pltpu.roll(by_lag, width - off, 1)
        toep_ref[0, s * p:(s + 1) * p, :] = win[:, :chunk * p].astype(BF16)
    fwd = lax.broadcasted_iota(jnp.int32, (p, LANES), 1) < n_state
    for s in range(chunk):
        rows = slice(s * p, (s + 1) * p)
        a_r, a_i = pwi_ref[0, 0, s:s + 1, :], pwi_ref[0, 1, s:s + 1, :]
        w1_ref[0, rows, :LANES] = (bt_re * a_r - bt_im * a_i).astype(BF16)
        w1_ref[0, rows, LANES:] = (bt_re * a_i + bt_im * a_r).astype(BF16)
        a_r, a_i = pwo_ref[0, 0, s:s + 1, :], pwo_ref[0, 1, s:s + 1, :]
        o_re = c_re * a_r - c_im * a_i
        o_im = -(c_re * a_i + c_im * a_r)
        w2_ref[0, 0, rows, :LANES] = jnp.where(fwd, o_re, 0.0).astype(BF16)
        w2_ref[0, 0, rows, LANES:] = jnp.where(fwd, o_im, 0.0).astype(BF16)
        w2_ref[0, 1, rows, :LANES] = jnp.where(fwd, 0.0, o_re).astype(BF16)
        w2_ref[0, 1, rows, LANES:] = jnp.where(fwd, 0.0, o_im).astype(BF16)


def ssm_tables(bt, cc, skip, sel, pw_lag, pw_in, pw_out, chunk):
    groups, _, p, lanes = bt.shape
    n_state = lanes // 2
    tp = chunk * p
    g4 = lambda g: (g, 0, 0, 0)
    blk4 = lambda a: pl.BlockSpec((1,) + a.shape[1:], g4)
    return pl.pallas_call(
        functools.partial(_ssm_tables_kernel, chunk=chunk, n_state=n_state),
        grid=(groups,),
        in_specs=[blk4(bt), blk4(cc), pl.BlockSpec((1, p, LANES), lambda g: (g, 0, 0)), _resident(sel.shape),
                  blk4(pw_lag), blk4(pw_in), blk4(pw_out)],
        out_specs=[pl.BlockSpec((1, tp, tp), lambda g: (g, 0, 0)), pl.BlockSpec((1, tp, 2 * lanes), lambda g: (g, 0, 0)),
                   pl.BlockSpec((1, 2, tp, 2 * lanes), g4)],
        out_shape=[jax.ShapeDtypeStruct((groups, tp, tp), BF16), jax.ShapeDtypeStruct((groups, tp, 2 * lanes), BF16),
                   jax.ShapeDtypeStruct((groups, 2, tp, 2 * lanes), BF16)],
        scratch_shapes=[pltpu.VMEM(sel.shape[:1] + (2 * lanes,), BF16)],
        compiler_params=_params(1),
        name="ssm_tables",
    )(bt, cc, skip, sel, pw_lag, pw_in, pw_out)


def lane_select(p):
    per = LANES // p
    r = jnp.arange(per * LANES)
    c = jnp.arange(2 * LANES)
    a, b, x = r // LANES, (r % LANES) // p, r % p
    h, a2, x2 = c // LANES, (c % LANES) // p, c % p
    k = jnp.arange(per // 2)[:, None, None]
    sel = (b[None, :, None] == 2 * k + h[None, None, :]) & (a[:, None] == a2[None, :]) & (x[:, None] == x2[None, :])
    return sel.astype(BF16)


def _to_rows_kernel(u_ref, sel_ref, o_ref, *, chunk):
    gps, rows, _ = o_ref.shape
    per = sel_ref.shape[1] // LANES
    for cb in range(chunk // per):
        cat = jnp.concatenate([u_ref[pl.ds(cb * per + s, rows, stride=chunk), :] for s in range(per)], axis=1)
        cat = cat.astype(BF16)
        for k in range(gps // 2):
            two = _dot(cat, sel_ref[k])
            o_ref[2 * k, :, cb * LANES:(cb + 1) * LANES] = two[:, :LANES].astype(o_ref.dtype)
            o_ref[2 * k + 1, :, cb * LANES:(cb + 1) * LANES] = two[:, LANES:].astype(o_ref.dtype)


def to_chunk_rows(u, sel, p, chunk, rows):
    t, width = u.shape
    gps = LANES // p
    return pl.pallas_call(
        functools.partial(_to_rows_kernel, chunk=chunk),
        grid=(width // LANES, t // (rows * chunk)),
        in_specs=[pl.BlockSpec((rows * chunk, LANES), lambda g, i: (i, g)), _resident(sel.shape)],
        out_specs=pl.BlockSpec((gps, rows, chunk * p), lambda g, i: (g, i, 0)),
        out_shape=jax.ShapeDtypeStruct((width // p, t // chunk, chunk * p), BF16),
        compiler_params=_params(2),
        name="to_chunk_rows",
    )(u, sel)


def _from_rows_kernel(y_ref, sel_ref, o_ref, *, chunk):
    gps, rows, _ = y_ref.shape
    per = sel_ref.shape[1] // LANES
    for cb in range(chunk // per):
        cat = jnp.concatenate([y_ref[g, :, cb * LANES:(cb + 1) * LANES] for g in range(gps)], axis=1)
        for k in range(per // 2):
            two = _dot(cat, sel_ref[k])
            o_ref[pl.ds(cb * per + 2 * k, rows, stride=chunk), :] = two[:, :LANES]
            o_ref[pl.ds(cb * per + 2 * k + 1, rows, stride=chunk), :] = two[:, LANES:]


def from_chunk_rows(y_rows, sel, p, chunk, rows):
    groups, n_rows, _ = y_rows.shape
    gps = LANES // p
    return pl.pallas_call(
        functools.partial(_from_rows_kernel, chunk=chunk),
        grid=(groups // gps, n_rows // rows),
        in_specs=[pl.BlockSpec((gps, rows, chunk * p), lambda g, i: (g, i, 0)), _resident(sel.shape)],
        out_specs=pl.BlockSpec((rows * chunk, LANES), lambda g, i: (i, g)),
        out_shape=jax.ShapeDtypeStruct((n_rows * chunk, groups * p), F32),
        compiler_params=_params(2),
        name="from_chunk_rows",
    )(y_rows, sel)


def _mix_kernel(xn_ref, o_ref, y_ref, wga_ref, wgs_ref, wo_ref, wz1_ref, wz2_ref, mix_ref, gy_ref):
    @pl.when(pl.program_id(1) == 0)
    def _():
        gy_ref[...] = jax.nn.gelu(y_ref[...].astype(F32)).astype(BF16)

    xn = xn_ref[...]
    gy = gy_ref[...]
    o_a = _dot(o_ref[...], wo_ref[...])
    o_s = _dot(gy, wz1_ref[...]) * jax.nn.sigmoid(_dot(gy, wz2_ref[...]))
    g_a = jax.nn.sigmoid(_dot(xn, wga_ref[...]))
    g_s = jax.nn.sigmoid(_dot(xn, wgs_ref[...]))
    mix_ref[...] = (g_a * o_a + g_s * o_s).astype(BF16)


def mix_merge(xn, o, y, wga, wgs, wo, wz1, wz2, tm, tn):
    t, d = xn.shape
    ho, sw = o.shape[1], y.shape[1]
    row = lambda i, j: (i, 0)
    col = lambda i, j: (0, j)
    return pl.pallas_call(
        _mix_kernel,
        grid=(t // tm, d // tn),
        in_specs=[pl.BlockSpec((tm, d), row), pl.BlockSpec((tm, ho), row), pl.BlockSpec((tm, sw), row),
                  pl.BlockSpec((d, tn), col), pl.BlockSpec((d, tn), col), pl.BlockSpec((ho, tn), col),
                  pl.BlockSpec((sw, tn), col), pl.BlockSpec((sw, tn), col)],
        out_specs=pl.BlockSpec((tm, tn), lambda i, j: (i, j)),
        out_shape=jax.ShapeDtypeStruct((t, d), BF16),
        scratch_shapes=[pltpu.VMEM((tm, sw), BF16)],
        compiler_params=_params(2, 1),
        name="mix_merge",
    )(xn, o, y, wga, wgs, wo, wz1, wz2)


def _resid_norm_kernel(x_ref, mix_ref, w_ref, g_ref, h_ref, n_ref):
    h = x_ref[...] + _dot(mix_ref[...], w_ref[...])
    h_ref[...] = h
    n_ref[...] = _rms(h, g_ref[...]).astype(BF16)


def resid_norm(x, mix, w, g, tm):
    t, d = x.shape
    row = lambda i: (i, 0)
    return pl.pallas_call(
        _resid_norm_kernel,
        grid=(t // tm,),
        in_specs=[pl.BlockSpec((tm, d), row), pl.BlockSpec((tm, d), row), _resident((d, d)), _resident((1, d))],
        out_specs=[pl.BlockSpec((tm, d), row), pl.BlockSpec((tm, d), row)],
        out_shape=[jax.ShapeDtypeStruct((t, d), F32), jax.ShapeDtypeStruct((t, d), BF16)],
        compiler_params=_params(1),
        name="resid_norm",
    )(x, mix, w, g.reshape(1, d))


def _mlp_kernel(n_ref, h_ref, w1_ref, w2_ref, o_ref, acc_ref):
    j = pl.program_id(1)

    @pl.when(j == 0)
    def _():
        acc_ref[...] = h_ref[...]

    a = jnp.maximum(_dot(n_ref[...], w1_ref[...]), 0.0)
    acc_ref[...] += _dot((a * a).astype(BF16), w2_ref[...])

    @pl.when(j == pl.num_programs(1) - 1)
    def _():
        o_ref[...] = acc_ref[...]


def mlp(n, h, w1, w2, tm, tf):
    t, d = h.shape
    ff = w1.shape[1]
    row = lambda i, j: (i, 0)
    return pl.pallas_call(
        _mlp_kernel,
        grid=(t // tm, ff // tf),
        in_specs=[pl.BlockSpec((tm, d), row), pl.BlockSpec((tm, d), row),
                  pl.BlockSpec((d, tf), lambda i, j: (0, j)), pl.BlockSpec((tf, d), lambda i, j: (j, 0))],
        out_specs=pl.BlockSpec((tm, d), row),
        out_shape=jax.ShapeDtypeStruct((t, d), F32),
        scratch_shapes=[pltpu.VMEM((tm, d), F32)],
        compiler_params=_params(2, 1),
        name="mlp",
    )(n, h, w1, w2)


def _ple_kernel(h_ref, p_ref, wg_ref, wp_ref, gp_ref, gf_ref, o_ref, *, final):
    h = h_ref[...]
    gate = jax.nn.sigmoid(_dot(_rms(h, gp_ref[...]).astype(BF16), wg_ref[...]))
    h = h + gate * _dot(p_ref[...].astype(BF16), wp_ref[...])
    o_ref[...] = _rms(h, gf_ref[...]) if final else h


def ple(h, p, wg, wp, gp, gf, tm, final):
    t, d = h.shape
    pd = p.shape[1]
    row = lambda i: (i, 0)
    return pl.pallas_call(
        functools.partial(_ple_kernel, final=final),
        grid=(t // tm,),
        in_specs=[pl.BlockSpec((tm, d), row), pl.BlockSpec((tm, pd), row), _resident((d, d)), _resident((pd, d)),
                  _resident((1, d)), _resident((1, d))],
        out_specs=pl.BlockSpec((tm, d), row),
        out_shape=jax.ShapeDtypeStruct((t, d), F32),
        compiler_params=_params(1),
        name="ple",
    )(h, p, wg, wp, gp.reshape(1, d), gf.reshape(1, d))


def _rope_tables(length, rope_dim):
    pos = jnp.arange(length, dtype=F32)
    inv = ROPE_THETA ** (-jnp.arange(0, rope_dim, 2, dtype=F32) / rope_dim)
    ang = pos[:, None] * inv[None, :]
    return jnp.cos(ang), jnp.sin(ang)


def _rope_lanes(cos, sin):
    length, half = cos.shape
    z = jnp.zeros((length, LANES // 2 - half), F32)
    return (jnp.concatenate([cos, z, cos, z], axis=1), jnp.concatenate([-sin, z, sin, z], axis=1))


def _spread_rope(w, half):
    z = jnp.zeros(w.shape[:-1] + (LANES // 2 - half,), w.dtype)
    return jnp.concatenate([w[..., :half], z, w[..., half:], z], axis=-1)


def _tile(n, pref):
    t = min(n, pref)
    while n % t:
        t //= 2
    return t


def _layer_weights(w_in, norm_q, w_uq, norm_kv, w_uk, w_uv, w_o_attn, ssm, w_glu, d_model):
    cq, ckv = norm_q.shape[0], norm_kv.shape[0]
    n_heads, dn = w_uk.shape[1], w_uk.shape[2]
    dv = w_uv.shape[2]
    sw = ssm[7].shape[0]
    rope = w_in.shape[1] - (cq + ckv + sw + 2 * d_model)
    half = rope // 2
    o_q, o_kv, o_kr, o_u, o_ga, o_gs = 0, cq, cq + ckv, cq + ckv + rope, cq + ckv + rope + sw, cq + ckv + rope + sw + d_model
    scale = (dn + rope) ** -0.5 * math.log2(math.e)
    wuq = w_uq.reshape(cq, n_heads, dn + rope) * scale
    wuq_t = jnp.concatenate([wuq[..., :dn], _spread_rope(wuq[..., dn:], half)], axis=-1).transpose(1, 2, 0)
    return dict(
        rope=rope, n_heads=n_heads, dn=dn, dv=dv,
        wq=w_in[:, o_q:o_kv].astype(BF16), wkv=w_in[:, o_kv:o_kr].astype(BF16),
        wkr=_spread_rope(w_in[:, o_kr:o_u], half).astype(BF16),
        wu=w_in[:, o_u:o_ga].astype(BF16), wga=w_in[:, o_ga:o_gs].astype(BF16), wgs=w_in[:, o_gs:].astype(BF16),
        wuq_t=wuq_t.astype(BF16),
        wuk=w_uk.reshape(ckv, n_heads * dn).astype(BF16),
        wuv_t=w_uv.reshape(ckv, n_heads * dv).T.astype(BF16),
        wo=w_o_attn.astype(BF16),
        wz1=w_glu[:, :d_model].astype(BF16), wz2=w_glu[:, d_model:].astype(BF16),
        ssm_ops=ssm_operators(*ssm, SSM_CHUNK),
        lane_sel=lane_select(ssm[3].shape[-1]),
    )


def _encoder_layer(h, p, lw, norm_mix, norm_q, norm_kv, w_out, norm_mlp, w_mlp_in, w_mlp_out,
                   norm_ple, w_ple_gate, w_ple, norm_out, final, batch, seq_len):
    t, d = h.shape
    tm = _tile(seq_len, 512)
    cos, sin = _rope_tables(seq_len, lw['rope'])
    cos_l, sin_l = _rope_lanes(cos, sin)
    xn, qn, ckv, kr, u = in_proj(h, norm_mix, lw['wq'], lw['wkv'], lw['wkr'], lw['wu'], norm_q, norm_kv,
                                 cos_l, sin_l, seq_len, tm)
    qt, k, vt = attn_prep(qn, ckv, kr, lw['wuq_t'], lw['wuk'], lw['wuv_t'], cos_l.T, sin_l.T, batch, seq_len, tm)
    o = flash_attn(qt, k, vt, lw['dv'], _tile(seq_len, FLASH_TQ), _tile(seq_len, FLASH_TK)).reshape(t, -1)
    toep, w1, w2, dec = lw['ssm_ops']
    groups = toep.shape[0]
    gp = lw['wu'].shape[1] // groups
    n_chunk = seq_len // SSM_CHUNK
    rows_tile = _tile(batch * n_chunk, 256)
    u_rows = to_chunk_rows(u, lw['lane_sel'], gp, SSM_CHUNK, rows_tile)
    y_rows = ssm(u_rows, toep, w1, w2, dec, batch, n_chunk)
    y = from_chunk_rows(y_rows, lw['lane_sel'], gp, SSM_CHUNK, rows_tile)
    mix = mix_merge(xn, o, y, lw['wga'], lw['wgs'], lw['wo'], lw['wz1'], lw['wz2'], tm, _tile(d, 512))
    h1, n1 = resid_norm(h, mix, w_out.astype(BF16), norm_mlp, tm)
    h2 = mlp(n1, h1, w_mlp_in.astype(BF16), w_mlp_out.astype(BF16), tm, _tile(w_mlp_in.shape[1], 1024))
    return ple(h2, p, w_ple_gate.astype(BF16), w_ple.astype(BF16), norm_ple, norm_out, tm, final)


def kernel(x_prompt, x_sample, p_prompt, p_sample, norm_mix, w_in, norm_q, w_uq, norm_kv, w_uk, w_uv, w_o_attn, ssm_a_re, ssm_a_im, ssm_log_step, ssm_b_re, ssm_b_im, ssm_c_re, ssm_c_im, ssm_d, w_glu, w_out, norm_mlp, w_mlp_in, w_mlp_out, norm_ple, w_ple_gate, w_ple, norm_final):
    depth = w_in.shape[0]
    d_model = x_prompt.shape[-1]
    layers = []
    for i in range(depth):
        ssm_w = (ssm_a_re[i], ssm_a_im[i], ssm_log_step[i], ssm_b_re[i], ssm_b_im[i], ssm_c_re[i], ssm_c_im[i], ssm_d[i])
        layers.append(_layer_weights(w_in[i], norm_q[i], w_uq[i], norm_kv[i], w_uk[i], w_uv[i], w_o_attn[i],
                                     ssm_w, w_glu[i], d_model))
    outs = []
    for x, p in ((x_prompt, p_prompt), (x_sample, p_sample)):
        batch, seq_len, _ = x.shape
        h = x.reshape(batch * seq_len, d_model)
        for i in range(depth):
            final = i == depth - 1
            h = _encoder_layer(h, p[i].reshape(batch * seq_len, -1), layers[i], norm_mix[i], norm_q[i], norm_kv[i],
                               w_out[i], norm_mlp[i], w_mlp_in[i], w_mlp_out[i], norm_ple[i], w_ple_gate[i], w_ple[i],
                               norm_final if final else norm_ple[i], final, batch, seq_len)
        outs.append(h.reshape(batch, seq_len, d_model))
    return tuple(outs)
```

```python
import functools
import math

import jax
import jax.numpy as jnp
from jax import lax
from jax.experimental import pallas as pl
from jax.experimental.pallas import tpu as pltpu

NORM_EPS = 1e-6
ROPE_THETA = 10000.0
LANES = 128
SUBLANES = 8
BF16_ROWS = 16
FLASH_TQ = 2048
FLASH_TK = 512
FLASH_UNROLL = 4
MXU_DIM = 256
VMEM_LIMIT = 56 * 1024 * 1024
SSM_CHUNK = 64
BF16 = jnp.bfloat16
F32 = jnp.float32


def _params(n_axes, n_parallel=None):
    n_parallel = n_axes if n_parallel is None else n_parallel
    sem = ("parallel",) * n_parallel + ("arbitrary",) * (n_axes - n_parallel)
    return pltpu.CompilerParams(dimension_semantics=sem, vmem_limit_bytes=VMEM_LIMIT)


def _resident(shape):
    nd = len(shape)
    return pl.BlockSpec(shape, lambda *_: (0,) * nd, pipeline_mode=pl.Buffered(1))


def _rms(x, g):
    r = lax.rsqrt(jnp.mean(x * x, axis=-1, keepdims=True) + NORM_EPS)
    return (x * r) * g


def _dot(a, b):
    return jnp.dot(a, b, preferred_element_type=F32)


def _dot_nt(a, b):
    return lax.dot_general(a, b, (((1,), (1,)), ((), ())), preferred_element_type=F32)


def _in_proj_kernel(x_ref, g_ref, wq_ref, wkv_ref, wkr_ref, wu_ref, gq_ref, gkv_ref, cos_ref, sin_ref,
                    xn_ref, qn_ref, ckv_ref, kr_ref, u_ref):
    xn = _rms(x_ref[...], g_ref[...]).astype(BF16)
    xn_ref[...] = xn
    qn_ref[...] = _rms(_dot(xn, wq_ref[...]), gq_ref[...]).astype(BF16)
    ckv_ref[...] = _rms(_dot(xn, wkv_ref[...]), gkv_ref[...]).astype(BF16)
    kr = _dot(xn, wkr_ref[...])
    kr_ref[...] = (kr * cos_ref[...] + pltpu.roll(kr, LANES // 2, 1) * sin_ref[...]).astype(BF16)
    u_ref[...] = _dot(xn, wu_ref[...])


def in_proj(x, g, wq, wkv, wkr, wu, gq, gkv, cos_l, sin_l, seq_len, tm):
    t, d = x.shape
    cq, ckv, sw = wq.shape[1], wkv.shape[1], wu.shape[1]
    npos = seq_len // tm
    row = lambda i: (i, 0)
    pos = lambda i: (i % npos, 0)
    widths = (d, cq, ckv, LANES, sw)
    return pl.pallas_call(
        _in_proj_kernel,
        grid=(t // tm,),
        in_specs=[pl.BlockSpec((tm, d), row), _resident((1, d)),
                  _resident((d, cq)), _resident((d, ckv)), _resident((d, LANES)), _resident((d, sw)),
                  _resident((1, cq)), _resident((1, ckv)),
                  pl.BlockSpec((tm, LANES), pos), pl.BlockSpec((tm, LANES), pos)],
        out_specs=[pl.BlockSpec((tm, w), row) for w in widths],
        out_shape=[jax.ShapeDtypeStruct((t, w), BF16) for w in widths[:-1]] + [jax.ShapeDtypeStruct((t, sw), F32)],
        compiler_params=_params(1),
        name="in_proj",
    )(x, g.reshape(1, d), wq, wkv, wkr, wu, gq.reshape(1, cq), gkv.reshape(1, ckv), cos_l, sin_l)


def _attn_prep_kernel(qn_ref, ckv_ref, kr_ref, wuq_ref, wuk_ref, wuv_ref, cos_ref, sin_ref,
                      qt_ref, k_ref, vt_ref, *, n_heads, dn, dv):
    qn = qn_ref[...]
    ckv = ckv_ref[...]
    cos_t = cos_ref[...]
    sin_t = sin_ref[...]
    half = LANES // 2
    for h in range(n_heads):
        q = _dot_nt(wuq_ref[h], qn)
        qr = q[dn:]
        qr_sw = jnp.concatenate([qr[half:], qr[:half]], axis=0)
        qt_ref[0, h, :dn, :] = q[:dn].astype(BF16)
        qt_ref[0, h, dn:, :] = (qr * cos_t + qr_sw * sin_t).astype(BF16)
    k_all = _dot(ckv, wuk_ref[...])
    kr = kr_ref[...]
    for h in range(n_heads):
        k_ref[0, h, :, :dn] = k_all[:, h * dn:(h + 1) * dn].astype(BF16)
        k_ref[0, h, :, dn:] = kr
    vt = _dot_nt(wuv_ref[...], ckv).astype(BF16)
    tm = vt.shape[1]
    ones_rows = (lax.broadcasted_iota(jnp.int32, (BF16_ROWS, tm), 0) == 0).astype(BF16)
    dve = dv + BF16_ROWS
    for h in range(n_heads):
        vt_ref[0, h * dve:h * dve + dv, :] = vt[h * dv:(h + 1) * dv]
        vt_ref[0, h * dve + dv:(h + 1) * dve, :] = ones_rows


def attn_prep(qn, ckv, kr, wuq_t, wuk, wuv_t, cos_t, sin_t, batch, seq_len, tm):
    n_heads, dk, cq = wuq_t.shape
    dn = dk - LANES
    dv = wuv_t.shape[0] // n_heads
    hdv = n_heads * (dv + BF16_ROWS)
    ckv_dim = ckv.shape[1]
    npos = seq_len // tm
    row = lambda b, i: (b * npos + i, 0)
    col = lambda b, i: (0, i)
    return pl.pallas_call(
        functools.partial(_attn_prep_kernel, n_heads=n_heads, dn=dn, dv=dv),
        grid=(batch, npos),
        in_specs=[pl.BlockSpec((tm, cq), row), pl.BlockSpec((tm, ckv_dim), row), pl.BlockSpec((tm, LANES), row),
                  _resident((n_heads, dk, cq)), _resident((ckv_dim, n_heads * dn)), _resident(wuv_t.shape),
                  pl.BlockSpec((LANES, tm), col), pl.BlockSpec((LANES, tm), col)],
        out_specs=[pl.BlockSpec((1, n_heads, dk, tm), lambda b, i: (b, 0, 0, i)),
                   pl.BlockSpec((1, n_heads, tm, dk), lambda b, i: (b, 0, i, 0)),
                   pl.BlockSpec((1, hdv, tm), lambda b, i: (b, 0, i))],
        out_shape=[jax.ShapeDtypeStruct((batch, n_heads, dk, seq_len), BF16),
                   jax.ShapeDtypeStruct((batch, n_heads, seq_len, dk), BF16),
                   jax.ShapeDtypeStruct((batch, hdv, seq_len), BF16)],
        compiler_params=_params(2),
        name="attn_prep",
    )(qn, ckv, kr, wuq_t, wuk, wuv_t, cos_t, sin_t)


def _flash_kernel(qt_ref, k_ref, vt_ref, o_ref, m_ref, acc_ref, s0_ref, s1_ref, c0_ref, c1_ref,
                  *, tk, nk, dv, unroll):
    qt = qt_ref[0, 0]
    tq = qt.shape[1]
    m_ref[...] = jnp.full(m_ref.shape, -jnp.inf, F32)
    acc_ref[...] = jnp.zeros(acc_ref.shape, F32)

    col_blocks = [slice(c, c + MXU_DIM) for c in range(0, tq, MXU_DIM)]

    def scores(j, buf, cols):
        s_ref, c_ref = buf
        off = pl.multiple_of(j * tk, tk)
        s = _dot(k_ref[0, 0, pl.ds(off, tk), :], qt[:, cols])
        s_ref[:, cols] = s
        c_ref[:, cols] = jnp.max(s, axis=0, keepdims=True)

    def consume(j, buf, cols):
        s_ref, c_ref = buf
        off = pl.multiple_of(j * tk, tk)
        m_prev = m_ref[:, cols]
        m_new = jnp.maximum(m_prev, c_ref[:, cols])
        alpha = jnp.exp2(m_prev - m_new)
        p = jnp.exp2(s_ref[:, cols] - m_new).astype(BF16)
        acc_ref[:, cols] = alpha * acc_ref[:, cols] + _dot(vt_ref[0, :, pl.ds(off, tk)], p)
        m_ref[:, cols] = m_new

    bufs = ((s0_ref, c0_ref), (s1_ref, c1_ref))
    for cols in col_blocks:
        scores(0, bufs[0], cols)

    def group(base, last):
        for u in range(unroll):
            for cols in col_blocks:
                if not (last and u == unroll - 1):
                    scores(base + u + 1, bufs[(u + 1) % 2], cols)
                consume(base + u, bufs[u % 2], cols)

    def body(i, carry):
        group(i * unroll, False)
        return carry

    lax.fori_loop(0, nk // unroll - 1, body, 0)
    group(nk - unroll, True)
    acc = acc_ref[...]
    o_ref[0] = (acc[:dv] / acc[dv:dv + 1]).T.astype(BF16)


def flash_attn(qt, k, vt, dv, tq, tk):
    batch, n_heads, dk, seq_len = qt.shape
    dve = vt.shape[1] // n_heads
    nk = seq_len // tk
    unroll = min(FLASH_UNROLL, nk)
    assert unroll % 2 == 0 and nk % unroll == 0
    return pl.pallas_call(
        functools.partial(_flash_kernel, tk=tk, nk=nk, dv=dv, unroll=unroll),
        grid=(batch, n_heads, seq_len // tq),
        in_specs=[pl.BlockSpec((1, 1, dk, tq), lambda b, h, i: (b, h, 0, i)),
                  pl.BlockSpec((1, 1, seq_len, dk), lambda b, h, i: (b, h, 0, 0)),
                  pl.BlockSpec((1, dve, seq_len), lambda b, h, i: (b, h, 0))],
        out_specs=pl.BlockSpec((1, tq, dv), lambda b, h, i: (b, i, h)),
        out_shape=jax.ShapeDtypeStruct((batch, seq_len, n_heads * dv), BF16),
        scratch_shapes=[pltpu.VMEM((1, tq), F32), pltpu.VMEM((dve, tq), F32),
                        pltpu.VMEM((tk, tq), F32), pltpu.VMEM((tk, tq), F32),
                        pltpu.VMEM((1, tq), F32), pltpu.VMEM((1, tq), F32)],
        compiler_params=_params(3),
        name="flash_attn",
    )(qt, k, vt)


def _rows_to_block(rows):
    n = rows[0].shape[1]
    rid = lax.broadcasted_iota(jnp.int32, (SUBLANES, n), 0)
    blk = jnp.broadcast_to(rows[-1], (SUBLANES, n))
    for k in range(SUBLANES - 2, -1, -1):
        blk = jnp.where(rid == k, rows[k], blk)
    return blk


def _ssm_kernel(u_ref, toep_ref, w1_ref, w2_ref, dec_ref, y_ref, s_ref, xf_ref, xb_ref, *, n_seq, n_chunk, n_state):
    u = u_ref[0]
    s_ref[...] = _dot(u, w1_ref[0])
    dec = dec_ref[0]
    a_re, a_im = dec[0:1], dec[1:2]
    fwd_lane = lax.broadcasted_iota(jnp.int32, (1, LANES), 1) < n_state
    n_blk = n_chunk // SUBLANES

    def step(i, carry):
        new = []
        for q in range(n_seq):
            x_re, x_im = carry[2 * q], carry[2 * q + 1]
            rf = pl.multiple_of(q * n_chunk + i * SUBLANES, SUBLANES)
            rb = pl.multiple_of(q * n_chunk + (n_blk - 1 - i) * SUBLANES, SUBLANES)
            sf = s_ref[pl.ds(rf, SUBLANES), :]
            sb = s_ref[pl.ds(rb, SUBLANES), :]
            rows_re, rows_im = [], []
            for k in range(SUBLANES):
                kb = SUBLANES - 1 - k
                rows_re.append(x_re)
                rows_im.append(x_im)
                s_re = jnp.where(fwd_lane, sf[k:k + 1, :LANES], sb[kb:kb + 1, :LANES])
                s_im = jnp.where(fwd_lane, sf[k:k + 1, LANES:], sb[kb:kb + 1, LANES:])
                x_re, x_im = a_re * x_re - a_im * x_im + s_re, a_re * x_im + a_im * x_re + s_im
            xf_ref[pl.ds(rf, SUBLANES), :LANES] = _rows_to_block(rows_re)
            xf_ref[pl.ds(rf, SUBLANES), LANES:] = _rows_to_block(rows_im)
            xb_ref[pl.ds(rb, SUBLANES), :LANES] = _rows_to_block(rows_re[::-1])
            xb_ref[pl.ds(rb, SUBLANES), LANES:] = _rows_to_block(rows_im[::-1])
            new += [x_re, x_im]
        return tuple(new)

    zero = jnp.zeros((1, LANES), F32)
    lax.fori_loop(0, n_blk, step, (zero,) * (2 * n_seq))
    y_ref[0] = (_dot(u, toep_ref[0]) + _dot_nt(xf_ref[...].astype(BF16), w2_ref[0, 0])
                + _dot_nt(xb_ref[...].astype(BF16), w2_ref[0, 1])).astype(y_ref.dtype)


def ssm(u_rows, toep, w1, w2, dec, n_seq, n_chunk):
    groups, rows, tp = u_rows.shape
    n_state = w1.shape[2] // 4
    assert 2 * n_state == LANES and n_chunk % SUBLANES == 0
    g3 = lambda g: (g, 0, 0)
    return pl.pallas_call(
        functools.partial(_ssm_kernel, n_seq=n_seq, n_chunk=n_chunk, n_state=n_state),
        grid=(groups,),
        in_specs=[pl.BlockSpec((1, rows, tp), g3), pl.BlockSpec((1, tp, tp), g3),
                  pl.BlockSpec((1, tp, 4 * n_state), g3), pl.BlockSpec((1, 2, tp, 4 * n_state), lambda g: (g, 0, 0, 0)),
                  pl.BlockSpec((1, 2, LANES), g3)],
        out_specs=pl.BlockSpec((1, rows, tp), g3),
        out_shape=jax.ShapeDtypeStruct((groups, rows, tp), BF16),
        scratch_shapes=[pltpu.VMEM((rows, 4 * n_state), F32)] * 3,
        compiler_params=_params(1),
        name="ssm",
    )(u_rows, toep, w1, w2, dec)


def ssm_operators(a_re, a_im, log_step, b_re, b_im, c_re, c_im, d_skip, chunk):
    _, groups, n_state = a_re.shape
    p = b_re.shape[-1]
    step = jnp.exp(log_step)[..., None]
    mag = jnp.exp(step * a_re)
    ab_re, ab_im = mag * jnp.cos(step * a_im), mag * jnp.sin(step * a_im)
    den = a_re * a_re + a_im * a_im
    nr, ni = ab_re - 1.0, ab_im
    f_re = (nr * a_re + ni * a_im) / den
    f_im = (ni * a_re - nr * a_im) / den
    bb_re = f_re[..., None] * b_re - f_im[..., None] * b_im
    bb_im = f_re[..., None] * b_im + f_im[..., None] * b_re
    j = jnp.arange(chunk + 1, dtype=F32)[:, None, None, None]
    pw_mag = jnp.exp(j * (step * a_re)[None])
    pw_re = pw_mag * jnp.cos(j * (step * a_im)[None])
    pw_im = pw_mag * jnp.sin(j * (step * a_im)[None])
    def fb(x_f, x_b):
        return jnp.concatenate([x_f, x_b], axis=-1).transpose(1, 0, 2)

    def re_im(f):
        return jnp.stack([f(pw_re), f(pw_im)], axis=1)

    zero = jnp.zeros((chunk - 1, groups, n_state), F32)
    pw_lag = re_im(lambda w: fb(jnp.concatenate([zero, w[:chunk, 0]]), jnp.concatenate([w[chunk - 1::-1, 1], zero])))
    pw_lag = jnp.pad(pw_lag, ((0, 0), (0, 0), (0, 1), (0, 0)))
    pw_in = re_im(lambda w: fb(w[chunk - 1::-1, 0], w[:chunk, 1]))
    pw_out = re_im(lambda w: fb(w[1:chunk + 1, 0], w[chunk:0:-1, 1]))
    dec = jnp.stack([fb(pw_re[chunk:, 0], pw_re[chunk:, 1])[:, 0], fb(pw_im[chunk:, 0], pw_im[chunk:, 1])[:, 0]],
                    axis=1)
    bt = jnp.stack([jnp.concatenate([bb_re[0], bb_re[1]], axis=1), jnp.concatenate([bb_im[0], bb_im[1]], axis=1)],
                   axis=1).transpose(0, 1, 3, 2)
    cc = jnp.stack([jnp.concatenate([c_re[0], c_re[1]], axis=-1), jnp.concatenate([c_im[0], c_im[1]], axis=-1)],
                   axis=1)
    eye = jnp.eye(p, LANES, dtype=F32)
    skip = d_skip.reshape(groups, p)[:, :, None] * eye
    n_rows = -(-(2 * chunk - 1) * p // LANES) * LANES
    sel = jnp.zeros((n_rows, LANES), F32).at[(chunk - 1) * p:chunk * p].set(eye)
    toep, w1, w2 = ssm_tables(bt, cc, skip, sel.astype(BF16), pw_lag, pw_in, pw_out, chunk)
    return toep, w1, w2, dec


def _ssm_tables_kernel(bt_ref, c_ref, skip_ref, sel_ref, pwl_ref, pwi_ref, pwo_ref, toep_ref, w1_ref, w2_ref, rt_ref,
                       *, chunk, n_state):
    p = bt_ref.shape[2]
    bt_re, bt_im = bt_ref[0, 0], bt_ref[0, 1]
    c_re, c_im = c_ref[0, 0], c_ref[0, 1]
    rt_ref[...] = jnp.zeros(rt_ref.shape, BF16)
    for li in range(2 * chunk - 1):
        a_r, a_i = pwl_ref[0, 0, li:li + 1, :], pwl_ref[0, 1, li:li + 1, :]
        rt_ref[li * p:(li + 1) * p, :LANES] = (c_re * a_r - c_im * a_i).astype(BF16)
        rt_ref[li * p:(li + 1) * p, LANES:] = (c_re * a_i + c_im * a_r).astype(BF16)
    lhs = jnp.concatenate([bt_re, -bt_im], axis=1).astype(BF16)
    by_lag = _dot_nt(lhs, rt_ref[...]) + _dot_nt(skip_ref[0].astype(BF16), sel_ref[...])
    width = by_lag.shape[1]
    for s in range(chunk):
        off = (chunk - 1 - s) * p
        win = by_lag if off == 0 else pltpu.roll(by_lag, width - off, 1)
        toep_ref[0, s * p:(s + 1) * p, :] = win[:, :chunk * p].astype(BF16)
    fwd = lax.broadcasted_iota(jnp.int32, (p, LANES), 1) < n_state
    for s in range(chunk):
        rows = slice(s * p, (s + 1) * p)
        a_r, a_i = pwi_ref[0, 0, s:s + 1, :], pwi_ref[0, 1, s:s + 1, :]
        w1_ref[0, rows, :LANES] = (bt_re * a_r - bt_im * a_i).astype(BF16)
        w1_ref[0, rows, LANES:] = (bt_re * a_i + bt_im * a_r).astype(BF16)
        a_r, a_i = pwo_ref[0, 0, s:s + 1, :], pwo_ref[0, 1, s:s + 1, :]
        o_re = c_re * a_r - c_im * a_i
        o_im = -(c_re * a_i + c_im * a_r)
        w2_ref[0, 0, rows, :LANES] = jnp.where(fwd, o_re, 0.0).astype(BF16)
        w2_ref[0, 0, rows, LANES:] = jnp.where(fwd, o_im, 0.0).astype(BF16)
        w2_ref[0, 1, rows, :LANES] = jnp.where(fwd, 0.0, o_re).astype(BF16)
        w2_ref[0, 1, rows, LANES:] = jnp.where(fwd, 0.0, o_im).astype(BF16)


def ssm_tables(bt, cc, skip, sel, pw_lag, pw_in, pw_out, chunk):
    groups, _, p, lanes = bt.shape
    n_state = lanes // 2
    tp = chunk * p
    g4 = lambda g: (g, 0, 0, 0)
    blk4 = lambda a: pl.BlockSpec((1,) + a.shape[1:], g4)
    return pl.pallas_call(
        functools.partial(_ssm_tables_kernel, chunk=chunk, n_state=n_state),
        grid=(groups,),
        in_specs=[blk4(bt), blk4(cc), pl.BlockSpec((1, p, LANES), lambda g: (g, 0, 0)), _resident(sel.shape),
                  blk4(pw_lag), blk4(pw_in), blk4(pw_out)],
        out_specs=[pl.BlockSpec((1, tp, tp), lambda g: (g, 0, 0)), pl.BlockSpec((1, tp, 2 * lanes), lambda g: (g, 0, 0)),
                   pl.BlockSpec((1, 2, tp, 2 * lanes), g4)],
        out_shape=[jax.ShapeDtypeStruct((groups, tp, tp), BF16), jax.ShapeDtypeStruct((groups, tp, 2 * lanes), BF16),
                   jax.ShapeDtypeStruct((groups, 2, tp, 2 * lanes), BF16)],
        scratch_shapes=[pltpu.VMEM(sel.shape[:1] + (2 * lanes,), BF16)],
        compiler_params=_params(1),
        name="ssm_tables",
    )(bt, cc, skip, sel, pw_lag, pw_in, pw_out)


def lane_select(p):
    per = LANES // p
    r = jnp.arange(per * LANES)
    c = jnp.arange(2 * LANES)
    a, b, x = r // LANES, (r % LANES) // p, r % p
    h, a2, x2 = c // LANES, (c % LANES) // p, c % p
    k = jnp.arange(per // 2)[:, None, None]
    sel = (b[None, :, None] == 2 * k + h[None, None, :]) & (a[:, None] == a2[None, :]) & (x[:, None] == x2[None, :])
    return sel.astype(BF16)


def _to_rows_kernel(u_ref, sel_ref, o_ref, *, chunk):
    gps, rows, _ = o_ref.shape
    per = sel_ref.shape[1] // LANES
    for cb in range(chunk // per):
        cat = jnp.concatenate([u_ref[pl.ds(cb * per + s, rows, stride=chunk), :] for s in range(per)], axis=1)
        cat = cat.astype(BF16)
        for k in range(gps // 2):
            two = _dot(cat, sel_ref[k])
            o_ref[2 * k, :, cb * LANES:(cb + 1) * LANES] = two[:, :LANES].astype(o_ref.dtype)
            o_ref[2 * k + 1, :, cb * LANES:(cb + 1) * LANES] = two[:, LANES:].astype(o_ref.dtype)


def to_chunk_rows(u, sel, p, chunk, rows):
    t, width = u.shape
    gps = LANES // p
    return pl.pallas_call(
        functools.partial(_to_rows_kernel, chunk=chunk),
        grid=(width // LANES, t // (rows * chunk)),
        in_specs=[pl.BlockSpec((rows * chunk, LANES), lambda g, i: (i, g)), _resident(sel.shape)],
        out_specs=pl.BlockSpec((gps, rows, chunk * p), lambda g, i: (g, i, 0)),
        out_shape=jax.ShapeDtypeStruct((width // p, t // chunk, chunk * p), BF16),
        compiler_params=_params(2),
        name="to_chunk_rows",
    )(u, sel)


def _from_rows_kernel(y_ref, sel_ref, o_ref, *, chunk):
    gps, rows, _ = y_ref.shape
    per = sel_ref.shape[1] // LANES
    for cb in range(chunk // per):
        cat = jnp.concatenate([y_ref[g, :, cb * LANES:(cb + 1) * LANES] for g in range(gps)], axis=1)
        for k in range(per // 2):
            two = _dot(cat, sel_ref[k])
            o_ref[pl.ds(cb * per + 2 * k, rows, stride=chunk), :] = two[:, :LANES]
            o_ref[pl.ds(cb * per + 2 * k + 1, rows, stride=chunk), :] = two[:, LANES:]


def from_chunk_rows(y_rows, sel, p, chunk, rows):
    groups, n_rows, _ = y_rows.shape
    gps = LANES // p
    return pl.pallas_call(
        functools.partial(_from_rows_kernel, chunk=chunk),
        grid=(groups // gps, n_rows // rows),
        in_specs=[pl.BlockSpec((gps, rows, chunk * p), lambda g, i: (g, i, 0)), _resident(sel.shape)],
        out_specs=pl.BlockSpec((rows * chunk, LANES), lambda g, i: (i, g)),
        out_shape=jax.ShapeDtypeStruct((n_rows * chunk, groups * p), F32),
        compiler_params=_params(2),
        name="from_chunk_rows",
    )(y_rows, sel)


def _mix_kernel(xn_ref, o_ref, y_ref, wga_ref, wgs_ref, wo_ref, wz1_ref, wz2_ref, mix_ref, gy_ref):
    @pl.when(pl.program_id(1) == 0)
    def _():
        gy_ref[...] = jax.nn.gelu(y_ref[...].astype(F32)).astype(BF16)

    xn = xn_ref[...]
    gy = gy_ref[...]
    o_a = _dot(o_ref[...], wo_ref[...])
    o_s = _dot(gy, wz1_ref[...]) * jax.nn.sigmoid(_dot(gy, wz2_ref[...]))
    g_a = jax.nn.sigmoid(_dot(xn, wga_ref[...]))
    g_s = jax.nn.sigmoid(_dot(xn, wgs_ref[...]))
    mix_ref[...] = (g_a * o_a + g_s * o_s).astype(BF16)


def mix_merge(xn, o, y, wga, wgs, wo, wz1, wz2, tm, tn):
    t, d = xn.shape
    ho, sw = o.shape[1], y.shape[1]
    row = lambda i, j: (i, 0)
    col = lambda i, j: (0, j)
    return pl.pallas_call(
        _mix_kernel,
        grid=(t // tm, d // tn),
        in_specs=[pl.BlockSpec((tm, d), row), pl.BlockSpec((tm, ho), row), pl.BlockSpec((tm, sw), row),
                  pl.BlockSpec((d, tn), col), pl.BlockSpec((d, tn), col), pl.BlockSpec((ho, tn), col),
                  pl.BlockSpec((sw, tn), col), pl.BlockSpec((sw, tn), col)],
        out_specs=pl.BlockSpec((tm, tn), lambda i, j: (i, j)),
        out_shape=jax.ShapeDtypeStruct((t, d), BF16),
        scratch_shapes=[pltpu.VMEM((tm, sw), BF16)],
        compiler_params=_params(2, 1),
        name="mix_merge",
    )(xn, o, y, wga, wgs, wo, wz1, wz2)


def _resid_norm_kernel(x_ref, mix_ref, w_ref, g_ref, h_ref, n_ref):
    h = x_ref[...] + _dot(mix_ref[...], w_ref[...])
    h_ref[...] = h
    n_ref[...] = _rms(h, g_ref[...]).astype(BF16)


def resid_norm(x, mix, w, g, tm):
    t, d = x.shape
    row = lambda i: (i, 0)
    return pl.pallas_call(
        _resid_norm_kernel,
        grid=(t // tm,),
        in_specs=[pl.BlockSpec((tm, d), row), pl.BlockSpec((tm, d), row), _resident((d, d)), _resident((1, d))],
        out_specs=[pl.BlockSpec((tm, d), row), pl.BlockSpec((tm, d), row)],
        out_shape=[jax.ShapeDtypeStruct((t, d), F32), jax.ShapeDtypeStruct((t, d), BF16)],
        compiler_params=_params(1),
        name="resid_norm",
    )(x, mix, w, g.reshape(1, d))


def _mlp_kernel(n_ref, h_ref, w1_ref, w2_ref, o_ref, acc_ref):
    j = pl.program_id(1)

    @pl.when(j == 0)
    def _():
        acc_ref[...] = h_ref[...]

    a = jnp.maximum(_dot(n_ref[...], w1_ref[...]), 0.0)
    acc_ref[...] += _dot((a * a).astype(BF16), w2_ref[...])

    @pl.when(j == pl.num_programs(1) - 1)
    def _():
        o_ref[...] = acc_ref[...]


def mlp(n, h, w1, w2, tm, tf):
    t, d = h.shape
    ff = w1.shape[1]
    row = lambda i, j: (i, 0)
    return pl.pallas_call(
        _mlp_kernel,
        grid=(t // tm, ff // tf),
        in_specs=[pl.BlockSpec((tm, d), row), pl.BlockSpec((tm, d), row),
                  pl.BlockSpec((d, tf), lambda i, j: (0, j)), pl.BlockSpec((tf, d), lambda i, j: (j, 0))],
        out_specs=pl.BlockSpec((tm, d), row),
        out_shape=jax.ShapeDtypeStruct((t, d), F32),
        scratch_shapes=[pltpu.VMEM((tm, d), F32)],
        compiler_params=_params(2, 1),
        name="mlp",
    )(n, h, w1, w2)


def _ple_kernel(h_ref, p_ref, wg_ref, wp_ref, gp_ref, gf_ref, o_ref, *, final):
    h = h_ref[...]
    gate = jax.nn.sigmoid(_dot(_rms(h, gp_ref[...]).astype(BF16), wg_ref[...]))
    h = h + gate * _dot(p_ref[...].astype(BF16), wp_ref[...])
    o_ref[...] = _rms(h, gf_ref[...]) if final else h


def ple(h, p, wg, wp, gp, gf, tm, final):
    t, d = h.shape
    pd = p.shape[1]
    row = lambda i: (i, 0)
    return pl.pallas_call(
        functools.partial(_ple_kernel, final=final),
        grid=(t // tm,),
        in_specs=[pl.BlockSpec((tm, d), row), pl.BlockSpec((tm, pd), row), _resident((d, d)), _resident((pd, d)),
                  _resident((1, d)), _resident((1, d))],
        out_specs=pl.BlockSpec((tm, d), row),
        out_shape=jax.ShapeDtypeStruct((t, d), F32),
        compiler_params=_params(1),
        name="ple",
    )(h, p, wg, wp, gp.reshape(1, d), gf.reshape(1, d))


def _rope_tables(length, rope_dim):
    pos = jnp.arange(length, dtype=F32)
    inv = ROPE_THETA ** (-jnp.arange(0, rope_dim, 2, dtype=F32) / rope_dim)
    ang = pos[:, None] * inv[None, :]
    return jnp.cos(ang), jnp.sin(ang)


def _rope_lanes(cos, sin):
    length, half = cos.shape
    z = jnp.zeros((length, LANES // 2 - half), F32)
    return (jnp.concatenate([cos, z, cos, z], axis=1), jnp.concatenate([-sin, z, sin, z], axis=1))


def _spread_rope(w, half):
    z = jnp.zeros(w.shape[:-1] + (LANES // 2 - half,), w.dtype)
    return jnp.concatenate([w[..., :half], z, w[..., half:], z], axis=-1)


def _tile(n, pref):
    t = min(n, pref)
    while n % t:
        t //= 2
    return t


def _layer_weights(w_in, norm_q, w_uq, norm_kv, w_uk, w_uv, w_o_attn, ssm, w_glu, d_model):
    cq, ckv = norm_q.shape[0], norm_kv.shape[0]
    n_heads, dn = w_uk.shape[1], w_uk.shape[2]
    dv = w_uv.shape[2]
    sw = ssm[7].shape[0]
    rope = w_in.shape[1] - (cq + ckv + sw + 2 * d_model)
    half = rope // 2
    o_q, o_kv, o_kr, o_u, o_ga, o_gs = 0, cq, cq + ckv, cq + ckv + rope, cq + ckv + rope + sw, cq + ckv + rope + sw + d_model
    scale = (dn + rope) ** -0.5 * math.log2(math.e)
    wuq = w_uq.reshape(cq, n_heads, dn + rope) * scale
    wuq_t = jnp.concatenate([wuq[..., :dn], _spread_rope(wuq[..., dn:], half)], axis=-1).transpose(1, 2, 0)
    return dict(
        rope=rope, n_heads=n_heads, dn=dn, dv=dv,
        wq=w_in[:, o_q:o_kv].astype(BF16), wkv=w_in[:, o_kv:o_kr].astype(BF16),
        wkr=_spread_rope(w_in[:, o_kr:o_u], half).astype(BF16),
        wu=w_in[:, o_u:o_ga].astype(BF16), wga=w_in[:, o_ga:o_gs].astype(BF16), wgs=w_in[:, o_gs:].astype(BF16),
        wuq_t=wuq_t.astype(BF16),
        wuk=w_uk.reshape(ckv, n_heads * dn).astype(BF16),
        wuv_t=w_uv.reshape(ckv, n_heads * dv).T.astype(BF16),
        wo=w_o_attn.astype(BF16),
        wz1=w_glu[:, :d_model].astype(BF16), wz2=w_glu[:, d_model:].astype(BF16),
        ssm_ops=ssm_operators(*ssm, SSM_CHUNK),
        lane_sel=lane_select(ssm[3].shape[-1]),
    )


def _encoder_layer(h, p, lw, norm_mix, norm_q, norm_kv, w_out, norm_mlp, w_mlp_in, w_mlp_out,
                   norm_ple, w_ple_gate, w_ple, norm_out, final, batch, seq_len):
    t, d = h.shape
    tm = _tile(seq_len, 512)
    cos, sin = _rope_tables(seq_len, lw['rope'])
    cos_l, sin_l = _rope_lanes(cos, sin)
    xn, qn, ckv, kr, u = in_proj(h, norm_mix, lw['wq'], lw['wkv'], lw['wkr'], lw['wu'], norm_q, norm_kv,
                                 cos_l, sin_l, seq_len, tm)
    qt, k, vt = attn_prep(qn, ckv, kr, lw['wuq_t'], lw['wuk'], lw['wuv_t'], cos_l.T, sin_l.T, batch, seq_len, tm)
    o = flash_attn(qt, k, vt, lw['dv'], _tile(seq_len, FLASH_TQ), _tile(seq_len, FLASH_TK)).reshape(t, -1)
    toep, w1, w2, dec = lw['ssm_ops']
    groups = toep.shape[0]
    gp = lw['wu'].shape[1] // groups
    n_chunk = seq_len // SSM_CHUNK
    rows_tile = _tile(batch * n_chunk, 256)
    u_rows = to_chunk_rows(u, lw['lane_sel'], gp, SSM_CHUNK, rows_tile)
    y_rows = ssm(u_rows, toep, w1, w2, dec, batch, n_chunk)
    y = from_chunk_rows(y_rows, lw['lane_sel'], gp, SSM_CHUNK, rows_tile)
    mix = mix_merge(xn, o, y, lw['wga'], lw['wgs'], lw['wo'], lw['wz1'], lw['wz2'], tm, _tile(d, 512))
    h1, n1 = resid_norm(h, mix, w_out.astype(BF16), norm_mlp, tm)
    h2 = mlp(n1, h1, w_mlp_in.astype(BF16), w_mlp_out.astype(BF16), tm, _tile(w_mlp_in.shape[1], 1024))
    return ple(h2, p, w_ple_gate.astype(BF16), w_ple.astype(BF16), norm_ple, norm_out, tm, final)


def kernel(x_prompt, x_sample, p_prompt, p_sample, norm_mix, w_in, norm_q, w_uq, norm_kv, w_uk, w_uv, w_o_attn, ssm_a_re, ssm_a_im, ssm_log_step, ssm_b_re, ssm_b_im, ssm_c_re, ssm_c_im, ssm_d, w_glu, w_out, norm_mlp, w_mlp_in, w_mlp_out, norm_ple, w_ple_gate, w_ple, norm_final):
    depth = w_in.shape[0]
    d_model = x_prompt.shape[-1]
    layers = []
    for i in range(depth):
        ssm_w = (ssm_a_re[i], ssm_a_im[i], ssm_log_step[i], ssm_b_re[i], ssm_b_im[i], ssm_c_re[i], ssm_c_im[i], ssm_d[i])
        layers.append(_layer_weights(w_in[i], norm_q[i], w_uq[i], norm_kv[i], w_uk[i], w_uv[i], w_o_attn[i],
                                     ssm_w, w_glu[i], d_model))
    outs = []
    for x, p in ((x_prompt, p_prompt), (x_sample, p_sample)):
        batch, seq_len, _ = x.shape
        h = x.reshape(batch * seq_len, d_model)
        for i in range(depth):
            final = i == depth - 1
            h = _encoder_layer(h, p[i].reshape(batch * seq_len, -1), layers[i], norm_mix[i], norm_q[i], norm_kv[i],
                               w_out[i], norm_mlp[i], w_mlp_in[i], w_mlp_out[i], norm_ple[i], w_ple_gate[i], w_ple[i],
                               norm_final if final else norm_ple[i], final, batch, seq_len)
        outs.append(h.reshape(batch, seq_len, d_model))
    return tuple(outs)
```
